```python
import jax, jax.numpy as jnp
from jax import lax
import numpy as np

D_MODEL = 1024
BATCH = 8
SEQ = 8192
DEPTH = 4

CHUNK = 64
N_A_LAYERS = DEPTH // 2
N_B_LAYERS = DEPTH - N_A_LAYERS
A_HEADS = 8
A_KEY_DIM = 128
A_VAL_DIM = D_MODEL // A_HEADS
A_FORGET_DIM = A_HEADS * A_KEY_DIM
A_IN_DIM = 2 * A_FORGET_DIM + 2 * D_MODEL
B_HEADS = 16
B_HEAD_DIM = D_MODEL // B_HEADS
B_DIM = B_HEADS * B_HEAD_DIM
B_PAST_CHUNKS = 8
B_BAND = (B_PAST_CHUNKS + 1) * CHUNK
REL_CLIP = 256
N_REL = REL_CLIP + CHUNK
FFN_DIM = -(-8 * D_MODEL // (3 * 256)) * 256
NORM_EPS = 1e-6
N_MOD = 6
MASK_VALUE = -1e30
MIN_FORGET = 1e-30

kernel_name = "yoco_hgrn2_chunk_band_attention_trunk"


def rms_norm(x, gain):
    xf = x.astype(jnp.float32)
    y = xf * lax.rsqrt(jnp.mean(xf * xf, axis=-1, keepdims=True) + NORM_EPS)
    return (y * gain.astype(jnp.float32)).astype(x.dtype)


def modulate(x, shift, scale):
    return x * (1 + scale[:, None, :]) + shift[:, None, :]


def swiglu(u, w_in, w_out):
    gate, up = jnp.split(u @ w_in, 2, axis=-1)
    return (jax.nn.silu(gate) * up) @ w_out


def hgrn2_mixer(u, w_in, w_out, lb, o_gain):
    bsz, seq, _ = u.shape
    n_chunks = seq // CHUNK
    q, z_f, v, g = jnp.split(u @ w_in, [A_FORGET_DIM, 2 * A_FORGET_DIM, 2 * A_FORGET_DIM + D_MODEL], axis=-1)
    q = jax.nn.silu(q)
    zf = z_f.astype(jnp.float32)
    lbf = lb.astype(jnp.float32)
    sig = jax.nn.sigmoid(zf)
    f = lbf + (1 - lbf) * sig
    log_f = jnp.log(jnp.maximum(f, MIN_FORGET))
    k = (1 - lbf) * (1 - sig)

    def to_chunks(t, d):
        return t.astype(jnp.float32).reshape(bsz, n_chunks, CHUNK, A_HEADS, d).transpose(1, 0, 2, 3, 4)

    xs = (to_chunks(q, A_KEY_DIM), to_chunks(k, A_KEY_DIM), to_chunks(v, A_VAL_DIM), to_chunks(log_f, A_KEY_DIM))
    causal = jnp.tril(jnp.ones((CHUNK, CHUNK), dtype=bool))

    def step(state, chunk):
        qc, kc, vc, lfc = chunk
        b = jnp.cumsum(lfc, axis=1)
        diff = b[:, :, None] - b[:, None, :]
        decay = jnp.exp(jnp.where(causal[None, :, :, None, None], diff, MASK_VALUE))
        attn = jnp.einsum('bthd,btshd,bshd->bhts', qc, decay, kc)
        o = (jnp.einsum('bhts,bshv->bthv', attn, vc)
             + jnp.einsum('bthd,bhdv->bthv', qc * jnp.exp(b), state))
        b_last = b[:, -1]
        state = (jnp.exp(b_last)[..., None] * state
                 + jnp.einsum('bshd,bshv->bhdv', kc * jnp.exp(b_last[:, None] - b), vc))
        return state, o

    state0 = jnp.zeros((bsz, A_HEADS, A_KEY_DIM, A_VAL_DIM), jnp.float32)
    _, o = lax.scan(step, state0, xs)
    o = o.transpose(1, 0, 2, 3, 4).reshape(bsz, seq, A_HEADS, A_VAL_DIM)
    o = o * lax.rsqrt(jnp.mean(o * o, axis=-1, keepdims=True) + NORM_EPS) * o_gain.astype(jnp.float32)
    o = o * jax.nn.silu(g.astype(jnp.float32)).reshape(bsz, seq, A_HEADS, A_VAL_DIM)
    return o.reshape(bsz, seq, D_MODEL).astype(u.dtype) @ w_out


def shared_kv(h, c_act, gain, mod_w, mod_b, w_kv):
    bsz, seq, _ = h.shape
    shift, scale = jnp.split(c_act @ mod_w + mod_b, 2, axis=-1)
    u = modulate(rms_norm(h, gain), shift, scale)
    k, v = jnp.split(u @ w_kv, 2, axis=-1)
    pad = ((0, 0), (B_PAST_CHUNKS * CHUNK, 0), (0, 0), (0, 0))
    k = jnp.pad(k.reshape(bsz, seq, B_HEADS, B_HEAD_DIM), pad)
    v = jnp.pad(v.reshape(bsz, seq, B_HEADS, B_HEAD_DIM), pad)
    return k, v


def chunk_band_attention(u, k_pad, v_pad, w_q, w_o, rel_bias):
    bsz, seq, _ = u.shape
    n_chunks = seq // CHUNK
    q = (u @ w_q).reshape(bsz, n_chunks, CHUNK, B_HEADS, B_HEAD_DIM).transpose(1, 0, 2, 3, 4)
    q_pos = jnp.arange(CHUNK)[:, None] + B_PAST_CHUNKS * CHUNK
    k_pos = jnp.arange(B_BAND)
    rel = jnp.clip(k_pos[None, :] - q_pos, -REL_CLIP, CHUNK - 1) + REL_CLIP
    bias = rel_bias.astype(jnp.float32)[rel].transpose(2, 0, 1)
    scale = B_HEAD_DIM ** -0.5

    def one_chunk(args):
        qc, idx = args
        kb = lax.dynamic_slice_in_dim(k_pad, idx * CHUNK, B_BAND, axis=1)
        vb = lax.dynamic_slice_in_dim(v_pad, idx * CHUNK, B_BAND, axis=1)
        s = jnp.einsum('bqhd,bkhd->bhqk', qc, kb).astype(jnp.float32) * scale + bias
        valid = k_pos >= (B_PAST_CHUNKS - idx) * CHUNK
        s = jnp.where(valid, s, MASK_VALUE)
        p = jax.nn.softmax(s, axis=-1).astype(vb.dtype)
        return jnp.einsum('bhqk,bkhd->bqhd', p, vb)

    o = lax.map(one_chunk, (q, jnp.arange(n_chunks)))
    o = o.transpose(1, 0, 2, 3, 4).reshape(bsz, seq, B_DIM)
    return o @ w_o


def setup_inputs(seed: int = 0) -> dict:
    key = jax.random.key(seed)
    ks = jax.random.split(key, 24)
    f32 = jnp.float32
    nrm = lambda k, shape, s: jax.random.normal(k, shape, f32) * s
    D = D_MODEL
    return {
        "x": nrm(ks[0], (BATCH, SEQ, D), 1.0),
        "c": nrm(ks[1], (BATCH, D), 1.0),
        "mod_w": nrm(ks[2], (DEPTH, D, N_MOD * D), 0.5 * D ** -0.5),
        "mod_b": nrm(ks[3], (DEPTH, N_MOD * D), 0.02),
        "norm_mix": 1.0 + nrm(ks[4], (DEPTH, D), 0.02),
        "norm_ffn": 1.0 + nrm(ks[5], (DEPTH, D), 0.02),
        "ffn_w_in": nrm(ks[6], (DEPTH, D, 2 * FFN_DIM), D ** -0.5),
        "ffn_w_out": nrm(ks[7], (DEPTH, FFN_DIM, D), FFN_DIM ** -0.5),
        "a_w_in": nrm(ks[8], (N_A_LAYERS, D, A_IN_DIM), D ** -0.5),
        "a_w_out": nrm(ks[9], (N_A_LAYERS, D, D), D ** -0.5),
        "a_lb": nrm(ks[10], (N_A_LAYERS, A_FORGET_DIM), 1.0),
        "a_out_norm": 1.0 + nrm(ks[11], (N_A_LAYERS, A_VAL_DIM), 0.02),
        "kv_norm": 1.0 + nrm(ks[12], (D,), 0.02),
        "kv_mod_w": nrm(ks[13], (D, 2 * D), 0.5 * D ** -0.5),
        "kv_mod_b": nrm(ks[14], (2 * D,), 0.02),
        "kv_w": nrm(ks[15], (D, 2 * B_DIM), D ** -0.5),
        "b_w_q": nrm(ks[16], (N_B_LAYERS, D, B_DIM), D ** -0.5),
        "b_w_o": nrm(ks[17], (N_B_LAYERS, B_DIM, D), B_DIM ** -0.5),
        "b_rel_bias": nrm(ks[18], (N_B_LAYERS, N_REL, B_HEADS), 0.5),
        "final_norm": 1.0 + nrm(ks[19], (D,), 0.02),
    }


def reference(x, c, mod_w, mod_b, norm_mix, norm_ffn, ffn_w_in, ffn_w_out,
              a_w_in, a_w_out, a_lb, a_out_norm,
              kv_norm, kv_mod_w, kv_mod_b, kv_w,
              b_w_q, b_w_o, b_rel_bias, final_norm):
    c_act = jax.nn.silu(c)
    sm = jax.nn.softmax(a_lb.astype(jnp.float32), axis=0)
    lower_bounds = jnp.cumsum(sm, axis=0) - sm[0]
    h = x
    k_pad = None
    v_pad = None
    for layer in range(DEPTH):
        mod = c_act @ mod_w[layer] + mod_b[layer]
        sh1, sc1, g1, sh2, sc2, g2 = jnp.split(mod, N_MOD, axis=-1)
        u = modulate(rms_norm(h, norm_mix[layer]), sh1, sc1)
        if layer < N_A_LAYERS:
            mix = hgrn2_mixer(u, a_w_in[layer], a_w_out[layer], lower_bounds[layer], a_out_norm[layer])
        else:
            j = layer - N_A_LAYERS
            mix = chunk_band_attention(u, k_pad, v_pad, b_w_q[j], b_w_o[j], b_rel_bias[j])
        h = h + g1[:, None, :] * mix
        u = modulate(rms_norm(h, norm_ffn[layer]), sh2, sc2)
        h = h + g2[:, None, :] * swiglu(u, ffn_w_in[layer], ffn_w_out[layer])
        if layer == N_A_LAYERS - 1:
            k_pad, v_pad = shared_kv(h, c_act, kv_norm, kv_mod_w, kv_mod_b, kv_w)
    return rms_norm(h, final_norm)
```

```python
import functools

import numpy as np
import jax
import jax.numpy as jnp
from jax import lax
from jax.experimental import pallas as pl
from jax.experimental.pallas import tpu as pltpu

V7X_LANES = 128
V7X_SUBLANES = 8
V7X_VMEM_LIMIT_BYTES = 56 * 1024 * 1024

CHUNK = 64
A_HEADS = 8
A_KEY_DIM = 128
B_HEADS = 16
B_HEAD_DIM = 64
B_PAST_CHUNKS = 8
REL_CLIP = 256
NORM_EPS = 1e-6
N_MOD = 6
MASK_VALUE = -1e30
MIN_FORGET = 1e-30

SUB = V7X_SUBLANES
BLOCKS_PER_CHUNK = CHUNK // SUB
PAIR = 2 * A_KEY_DIM
N_OFF_PAIRS = BLOCKS_PER_CHUNK * (BLOCKS_PER_CHUNK - 1) // 2

HGRN_TILE = 256
ATT_TILE = 256
ATT_WIN = ATT_TILE + B_PAST_CHUNKS * CHUNK
ROW_TILE = 512
FFN_SPLIT = 2
MOD_MAX_TILE = 2048

f32 = jnp.float32
bf16 = jnp.bfloat16


def _params(*sem):
    return pltpu.CompilerParams(dimension_semantics=sem, vmem_limit_bytes=V7X_VMEM_LIMIT_BYTES)


def _resident(shape, index_map):
    return pl.BlockSpec(shape, index_map, pipeline_mode=pl.Buffered(1))


def _sigmoid(x):
    return 1.0 / (1.0 + jnp.exp(-x))


def _norm_mod(x, gain, shift, scale):
    y = x * lax.rsqrt(jnp.mean(x * x, axis=-1, keepdims=True) + NORM_EPS)
    return (y * gain) * (1.0 + scale) + shift


def _mod_kernel(c_ref, w_ref, b_ref, o_ref):
    c = c_ref[...]
    c_act = c * _sigmoid(c)
    o_ref[...] = jnp.dot(c_act, w_ref[...], preferred_element_type=f32,
                         precision=lax.Precision.HIGHEST) + b_ref[...]


def _modulation(c, w, b):
    n_layers, d, n = w.shape
    bsz = c.shape[0]
    tn = max(t for t in range(V7X_LANES, MOD_MAX_TILE + 1, V7X_LANES) if n % t == 0)
    return pl.pallas_call(
        _mod_kernel,
        grid=(n_layers, n // tn),
        in_specs=[
            pl.BlockSpec((bsz, d), lambda l, j: (0, 0)),
            pl.BlockSpec((None, d, tn), lambda l, j: (l, 0, j)),
            pl.BlockSpec((None, 1, tn), lambda l, j: (l, 0, j)),
        ],
        out_specs=pl.BlockSpec((None, bsz, tn), lambda l, j: (l, 0, j)),
        out_shape=jax.ShapeDtypeStruct((n_layers, bsz, n), f32),
        compiler_params=_params("arbitrary", "arbitrary"),
        name="modulation",
    )(c, w, b.reshape(n_layers, 1, n))


def _nmm_kernel(h_ref, gain_ref, shift_ref, scale_ref, w_ref, o_ref, *, out_scale):
    u = _norm_mod(h_ref[...], gain_ref[...], shift_ref[...], scale_ref[...])
    y = jnp.dot(u.astype(bf16), w_ref[...], preferred_element_type=f32)
    if out_scale != 1.0:
        y = y * out_scale
    o_ref[...] = y.astype(o_ref.dtype)


def _mod_spec(layer, col, d):
    return pl.BlockSpec((None, None, 1, d), lambda b, i: (layer, b, 0, col))


def _norm_mod_matmul(h, gain, mods, layer, shift_col, scale_col, w, w_layer, out_dtype,
                     out_scale=1.0):
    bsz, seq, d = h.shape
    n = w.shape[-1]
    tm = min(ROW_TILE, seq)
    return pl.pallas_call(
        functools.partial(_nmm_kernel, out_scale=out_scale),
        grid=(bsz, seq // tm),
        in_specs=[
            pl.BlockSpec((None, tm, d), lambda b, i: (b, i, 0)),
            pl.BlockSpec((1, d), lambda b, i: (0, 0)),
            _mod_spec(layer, shift_col, d),
            _mod_spec(layer, scale_col, d),
            _resident((None, d, n), lambda b, i: (w_layer, 0, 0)),
        ],
        out_specs=pl.BlockSpec((None, tm, n), lambda b, i: (b, i, 0)),
        out_shape=jax.ShapeDtypeStruct((bsz, seq, n), out_dtype),
        compiler_params=_params("parallel", "arbitrary"),
        name="norm_mod_matmul",
    )(h, gain.reshape(1, d), mods, mods, w)


def _proj_res_kernel(x_ref, w_ref, h_ref, gate_ref, o_ref):
    y = jnp.dot(x_ref[...], w_ref[...], preferred_element_type=f32)
    o_ref[...] = h_ref[...] + gate_ref[...] * y


def _proj_residual(x, w, w_layer, h, mods, layer, gate_col):
    bsz, seq, d = h.shape
    kdim = x.shape[-1]
    tm = min(ROW_TILE, seq)
    return pl.pallas_call(
        _proj_res_kernel,
        grid=(bsz, seq // tm),
        in_specs=[
            pl.BlockSpec((None, tm, kdim), lambda b, i: (b, i, 0)),
            _resident((None, kdim, d), lambda b, i: (w_layer, 0, 0)),
            pl.BlockSpec((None, tm, d), lambda b, i: (b, i, 0)),
            _mod_spec(layer, gate_col, d),
        ],
        out_specs=pl.BlockSpec((None, tm, d), lambda b, i: (b, i, 0)),
        out_shape=jax.ShapeDtypeStruct((bsz, seq, d), f32),
        compiler_params=_params("parallel", "arbitrary"),
        name="proj_residual",
    )(x, w, h, mods)


def _ffn_kernel(h_ref, gain_ref, shift_ref, scale_ref, gate_ref, win_ref, wout_ref, fgain_ref,
                o_ref, *, ffn_dim, final_norm):
    x = h_ref[...]
    u = _norm_mod(x, gain_ref[...], shift_ref[...], scale_ref[...]).astype(bf16)
    fc = ffn_dim // FFN_SPLIT
    acc = None
    for j in range(FFN_SPLIT):
        a = jnp.dot(u, win_ref[:, j * fc:(j + 1) * fc], preferred_element_type=f32)
        b = jnp.dot(u, win_ref[:, ffn_dim + j * fc:ffn_dim + (j + 1) * fc],
                    preferred_element_type=f32)
        act = ((a * _sigmoid(a)) * b).astype(bf16)
        p = jnp.dot(act, wout_ref[j * fc:(j + 1) * fc, :], preferred_element_type=f32)
        acc = p if acc is None else acc + p
    out = x + gate_ref[...] * acc
    if final_norm:
        out = out * lax.rsqrt(jnp.mean(out * out, axis=-1, keepdims=True) + NORM_EPS)
        out = out * fgain_ref[...]
    o_ref[...] = out


def _ffn(h, gain, mods, layer, w_in, w_out, final_gain, final_norm):
    bsz, seq, d = h.shape
    ffn_dim = w_out.shape[1]
    tm = min(ROW_TILE, seq)
    return pl.pallas_call(
        functools.partial(_ffn_kernel, ffn_dim=ffn_dim, final_norm=final_norm),
        grid=(bsz, seq // tm),
        in_specs=[
            pl.BlockSpec((None, tm, d), lambda b, i: (b, i, 0)),
            pl.BlockSpec((1, d), lambda b, i: (0, 0)),
            _mod_spec(layer, 3, d),
            _mod_spec(layer, 4, d),
            _mod_spec(layer, 5, d),
            _resident((None, d, 2 * ffn_dim), lambda b, i: (layer, 0, 0)),
            _resident((None, ffn_dim, d), lambda b, i: (layer, 0, 0)),
            pl.BlockSpec((1, d), lambda b, i: (0, 0)),
        ],
        out_specs=pl.BlockSpec((None, tm, d), lambda b, i: (b, i, 0)),
        out_shape=jax.ShapeDtypeStruct((bsz, seq, d), f32),
        compiler_params=_params("parallel", "arbitrary"),
        name="ffn",
    )(h, gain.reshape(1, d), mods, mods, mods, w_in, w_out, final_gain.reshape(1, d))


def _block_diag2(a, b):
    z = jnp.zeros_like(a)
    return jnp.concatenate([jnp.concatenate([a, z], axis=1),
                            jnp.concatenate([z, b], axis=1)], axis=0)


def _block_cumsum(x3):
    sub = lax.broadcasted_iota(jnp.int32, x3.shape, 1)
    for sh in (1, 2, 4):
        x3 = x3 + jnp.where(sub >= sh, pltpu.roll(x3, sh, axis=1), 0.0)
    return x3


def _hgrn_core_kernel(y_ref, alb_ref, gain_ref, wsel_ref, ones_ref, o_ref,
                      st_ref, q_scr, k_scr, c_scr, qt_scr, kt_scr, ctmp_scr, ctot_scr, ginc_scr, lhs_scr, diag_scr,
                      o_scr, *, layer, tile):
    d = A_HEADS * A_KEY_DIM
    nb = tile // SUB
    n_chunks = tile // CHUNK
    n_pairs = A_HEADS // 2

    @pl.when(pl.program_id(1) == 0)
    def _():
        st_ref[...] = jnp.zeros_like(st_ref)

    a_lb = alb_ref[...]
    e_lb = jnp.exp(a_lb - jnp.max(a_lb, axis=0, keepdims=True))
    sm = e_lb / jnp.sum(e_lb, axis=0, keepdims=True)
    lb = jnp.zeros((1, d), f32)
    for m in range(1, layer + 1):
        lb = lb + sm[m:m + 1, :]
    oml = 1.0 - lb

    qpre = y_ref[:, 0:d]
    zf = y_ref[:, d:2 * d]
    q = qpre * _sigmoid(qpre)
    sig = _sigmoid(zf)
    log_f = jnp.log(jnp.maximum(lb + oml * sig, MIN_FORGET))
    k = oml * (1.0 - sig)

    c3 = _block_cumsum(log_f.reshape(nb, SUB, d))
    ctot3 = c3[:, SUB - 1:SUB, :]
    q3 = q.reshape(nb, SUB, d)
    k3 = k.reshape(nb, SUB, d)
    q_scr[...] = q
    k_scr[...] = k
    c_scr[...] = c3.reshape(tile, d)
    qt_scr[...] = (q3 * jnp.exp(c3)).reshape(tile, d)
    kt_scr[...] = (k3 * jnp.exp(ctot3 - c3)).reshape(tile, d)

    for h in range(d // V7X_LANES):
        ctmp_scr[h] = c_scr[:, h * V7X_LANES:(h + 1) * V7X_LANES]
    ctot = jnp.concatenate([ctmp_scr[h, pl.ds(SUB - 1, nb, stride=SUB), :]
                            for h in range(d // V7X_LANES)], axis=1)
    ctot_scr[...] = ctot
    ginc_scr[...] = _block_cumsum(ctot.reshape(n_chunks, BLOCKS_PER_CHUNK, d)).reshape(nb, d)

    sub = lax.broadcasted_iota(jnp.int32, (nb, SUB, d), 1)
    for s in range(SUB):
        c3 = c_scr[...].reshape(nb, SUB, d)
        decay = jnp.exp(jnp.where(sub >= s, c3 - c3[:, s:s + 1, :], MASK_VALUE))
        p = (q_scr[...].reshape(nb, SUB, d) * decay) * k_scr[...].reshape(nb, SUB, d)[:, s:s + 1, :]
        p = p.reshape(tile, d).astype(bf16)
        for m in range(n_pairs):
            lhs_scr[m * tile:(m + 1) * tile, s * PAIR:(s + 1) * PAIR] = p[:, m * PAIR:(m + 1) * PAIR]
    diag_scr[...] = jnp.dot(lhs_scr[...], wsel_ref[...], preferred_element_type=f32)

    lane = lax.broadcasted_iota(jnp.int32, (SUB, V7X_LANES), 1)
    row = lax.broadcasted_iota(jnp.int32, (SUB, V7X_LANES), 0)
    col_block = (lane % CHUNK) // SUB
    col_sub = lane % SUB

    def chunk_body(ci, carry):
        r0 = pl.multiple_of(ci * CHUNK, CHUNK)
        b0 = pl.multiple_of(ci * BLOCKS_PER_CHUNK, BLOCKS_PER_CHUNK)
        qt = qt_scr[pl.ds(r0, CHUNK), :]
        kt = kt_scr[pl.ds(r0, CHUNK), :]
        v = y_ref[pl.ds(r0, CHUNK), 2 * d:3 * d]
        ginc = ginc_scr[pl.ds(b0, BLOCKS_PER_CHUNK), :]
        ctot_c = ctot_scr[pl.ds(b0, BLOCKS_PER_CHUNK), :]
        gexc = ginc - ctot_c
        btot = ginc[BLOCKS_PER_CHUNK - 1:BLOCKS_PER_CHUNK, :]
        e_g = jnp.exp(gexc)
        e_r = jnp.exp(btot - ginc)
        qh = jnp.concatenate([qt[SUB * i:SUB * (i + 1)] * e_g[i:i + 1] for i in range(BLOCKS_PER_CHUNK)],
                             axis=0).astype(bf16)
        kh = jnp.concatenate([kt[SUB * i:SUB * (i + 1)] * e_r[i:i + 1] for i in range(BLOCKS_PER_CHUNK)],
                             axis=0).astype(bf16)
        rows = []
        for i in range(1, BLOCKS_PER_CHUNK):
            d_i = jnp.exp(gexc[i:i + 1] - ginc[0:i])
            for j in range(i):
                rows.append(qt[SUB * i:SUB * (i + 1)] * d_i[j:j + 1])
        lhs_off = jnp.concatenate(rows, axis=0).astype(bf16)
        ktb = kt.astype(bf16)
        vb = v.astype(bf16)
        e_b = jnp.exp(btot)

        for m in range(n_pairs):
            lo, mid, hi = m * PAIR, m * PAIR + A_KEY_DIM, (m + 1) * PAIR
            k_bd = _block_diag2(ktb[:, lo:mid], ktb[:, mid:hi])
            off = lax.dot_general(lhs_off[:, lo:hi], k_bd, (((1,), (1,)), ((), ())),
                                  preferred_element_type=f32)
            dg = diag_scr[pl.ds(pl.multiple_of(m * tile + r0, CHUNK), CHUNK), :]
            blocks = []
            for i in range(BLOCKS_PER_CHUNK):
                a = jnp.where(col_block == i, jnp.where(col_sub <= row, dg[SUB * i:SUB * (i + 1)], 0.0), 0.0)
                for j in range(i):
                    pidx = i * (i - 1) // 2 + j
                    a = jnp.where(col_block == j, off[SUB * pidx:SUB * (pidx + 1)], a)
                blocks.append(a)
            a2 = jnp.concatenate(blocks, axis=0).astype(bf16)
            v_bd = _block_diag2(vb[:, lo:mid], vb[:, mid:hi])
            o_intra = jnp.dot(a2, v_bd, preferred_element_type=f32)
            st_e = st_ref[2 * m]
            st_o = st_ref[2 * m + 1]
            st_bd = _block_diag2(st_e.astype(bf16), st_o.astype(bf16))
            o_inter = lax.dot_general(qh[:, lo:hi], st_bd, (((1,), (1,)), ((), ())),
                                      preferred_element_type=f32)
            o_scr[pl.ds(r0, CHUNK), lo:hi] = o_intra + o_inter
            upd = lax.dot_general(vb[:, lo:hi], kh[:, lo:hi], (((0,), (0,)), ((), ())),
                                  preferred_element_type=f32)
            st_ref[2 * m] = e_b[:, lo:mid] * st_e + upd[:A_KEY_DIM, :A_KEY_DIM]
            st_ref[2 * m + 1] = e_b[:, mid:hi] * st_o + upd[A_KEY_DIM:, A_KEY_DIM:]
        return carry

    lax.fori_loop(0, n_chunks, chunk_body, 0)

    o = o_scr[...]
    osq = (o * o).astype(bf16)
    ms = jnp.concatenate(
        [jnp.dot(osq[:, m * PAIR:(m + 1) * PAIR], ones_ref[...], preferred_element_type=f32)
         for m in range(n_pairs)], axis=1) * (1.0 / A_KEY_DIM)
    g = y_ref[:, 3 * d:4 * d]
    out = o * lax.rsqrt(ms + NORM_EPS) * gain_ref[...]
    o_ref[...] = (out * (g * _sigmoid(g))).astype(o_ref.dtype)


def _hgrn_selector():
    r = np.arange(SUB * PAIR)
    s_local, h_r = r // PAIR, (r % PAIR) // A_KEY_DIM
    c = np.arange(2 * CHUNK)
    h_c, s_c = c // CHUNK, c % CHUNK
    sel = (h_r[:, None] == h_c[None, :]) & (s_local[:, None] == (s_c % SUB)[None, :])
    return jnp.asarray(sel, dtype=bf16)


def _hgrn_core(y, a_lb, o_gain, layer):
    bsz, seq, _ = y.shape
    d = A_HEADS * A_KEY_DIM
    tile = min(HGRN_TILE, seq)
    n_pairs = A_HEADS // 2
    ones_bd = jnp.asarray(np.kron(np.eye(2), np.ones((A_KEY_DIM, A_KEY_DIM))), dtype=bf16)
    gain_t = jnp.tile(o_gain.astype(f32), A_HEADS).reshape(1, d)
    n_a = a_lb.shape[0]
    return pl.pallas_call(
        functools.partial(_hgrn_core_kernel, layer=layer, tile=tile),
        grid=(bsz, seq // tile),
        in_specs=[
            pl.BlockSpec((None, tile, 4 * d), lambda b, i: (b, i, 0)),
            pl.BlockSpec((n_a, d), lambda b, i: (0, 0)),
            pl.BlockSpec((1, d), lambda b, i: (0, 0)),
            _resident((SUB * PAIR, 2 * CHUNK), lambda b, i: (0, 0)),
            _resident((PAIR, PAIR), lambda b, i: (0, 0)),
        ],
        out_specs=pl.BlockSpec((None, tile, d), lambda b, i: (b, i, 0)),
        out_shape=jax.ShapeDtypeStruct((bsz, seq, d), bf16),
        scratch_shapes=[
            pltpu.VMEM((A_HEADS, A_KEY_DIM, A_KEY_DIM), f32),
            pltpu.VMEM((tile, d), f32),
            pltpu.VMEM((tile, d), f32),
            pltpu.VMEM((tile, d), f32),
            pltpu.VMEM((tile, d), f32),
            pltpu.VMEM((tile, d), f32),
            pltpu.VMEM((d // V7X_LANES, tile, V7X_LANES), f32),
            pltpu.VMEM((tile // SUB, d), f32),
            pltpu.VMEM((tile // SUB, d), f32),
            pltpu.VMEM((n_pairs * tile, SUB * PAIR), bf16),
            pltpu.VMEM((n_pairs * tile, 2 * CHUNK), f32),
            pltpu.VMEM((tile, d), f32),
        ],
        compiler_params=_params("parallel", "arbitrary"),
        name="hgrn_core",
    )(y, a_lb.astype(f32), gain_t, _hgrn_selector(), ones_bd)


def _attn_kernel(q_ref, k0_ref, k1_ref, k2_ref, v0_ref, v1_ref, v2_ref, bias_ref, o_ref):
    i = pl.program_id(1)
    n_pairs = B_HEADS // 2
    lane = lax.broadcasted_iota(jnp.int32, (ATT_TILE, V7X_LANES), 1)
    even = lane < B_HEAD_DIM
    kcol = lax.broadcasted_iota(jnp.int32, (1, ATT_WIN), 1)
    pad_row = jnp.where(kcol >= (2 - i) * ATT_TILE, 0.0, MASK_VALUE)
    for m in range(n_pairs):
        lanes = slice(m * V7X_LANES, (m + 1) * V7X_LANES)
        q2 = q_ref[:, lanes]
        kwin = jnp.concatenate([k0_ref[:, lanes], k1_ref[:, lanes], k2_ref[:, lanes]], axis=0)
        vwin = jnp.concatenate([v0_ref[:, lanes], v1_ref[:, lanes], v2_ref[:, lanes]], axis=0)
        outs = []
        for par in range(2):
            qh = jnp.where(even if par == 0 else jnp.logical_not(even), q2, jnp.zeros_like(q2))
            s = lax.dot_general(qh, kwin, (((1,), (1,)), ((), ())), preferred_element_type=f32)
            s = s + bias_ref[2 * m + par] + pad_row
            p = jnp.exp(s - jnp.max(s, axis=-1, keepdims=True))
            denom = jnp.sum(p, axis=-1, keepdims=True)
            o2 = jnp.dot(p.astype(bf16), vwin, preferred_element_type=f32)
            outs.append(o2 * (1.0 / denom))
        o_ref[:, lanes] = jnp.where(even, outs[0], outs[1]).astype(o_ref.dtype)


def _attn_bias_table(rel_bias):
    qpos = np.arange(ATT_TILE)[:, None] + B_PAST_CHUNKS * CHUNK
    kpos = np.arange(ATT_WIN)[None, :]
    rel = np.clip(kpos - qpos, -REL_CLIP, CHUNK - 1) + REL_CLIP
    back = qpos // CHUNK - kpos // CHUNK
    in_band = (back >= 0) & (back <= B_PAST_CHUNKS)
    table = rel_bias.astype(f32)[jnp.asarray(rel)]
    table = jnp.where(jnp.asarray(in_band)[:, :, None], table, MASK_VALUE)
    return table.transpose(2, 0, 1)


def _attn_core(q, kv, rel_bias):
    bsz, seq, d = q.shape
    assert seq % ATT_TILE == 0
    bias = _attn_bias_table(rel_bias)

    def kv_spec(back, col):
        return pl.BlockSpec((None, ATT_TILE, d), lambda b, i: (b, jnp.maximum(i - back, 0), col))

    return pl.pallas_call(
        _attn_kernel,
        grid=(bsz, seq // ATT_TILE),
        in_specs=[
            pl.BlockSpec((None, ATT_TILE, d), lambda b, i: (b, i, 0)),
            kv_spec(2, 0), kv_spec(1, 0), kv_spec(0, 0),
            kv_spec(2, 1), kv_spec(1, 1), kv_spec(0, 1),
            _resident((B_HEADS, ATT_TILE, ATT_WIN), lambda b, i: (0, 0, 0)),
        ],
        out_specs=pl.BlockSpec((None, ATT_TILE, d), lambda b, i: (b, i, 0)),
        out_shape=jax.ShapeDtypeStruct((bsz, seq, d), bf16),
        compiler_params=_params("parallel", "arbitrary"),
        name="band_attention",
    )(q, kv, kv, kv, kv, kv, kv, bias)


def kernel(x, c, mod_w, mod_b, norm_mix, norm_ffn, ffn_w_in, ffn_w_out, a_w_in, a_w_out, a_lb,
           a_out_norm, kv_norm, kv_mod_w, kv_mod_b, kv_w, b_w_q, b_w_o, b_rel_bias, final_norm):
    depth = mod_w.shape[0]
    n_a = a_w_in.shape[0]
    bsz = x.shape[0]

    mods = _modulation(c, mod_w, mod_b).reshape(depth, bsz, 1, -1)
    kv_mods = _modulation(c, kv_mod_w[None], kv_mod_b[None]).reshape(1, bsz, 1, -1)

    ffn_w_in_b = ffn_w_in.astype(bf16)
    ffn_w_out_b = ffn_w_out.astype(bf16)
    a_w_in_b = a_w_in.astype(bf16)
    a_w_out_b = a_w_out.astype(bf16)
    kv_w_b = kv_w.astype(bf16)[None]
    b_w_q_b = b_w_q.astype(bf16)
    b_w_o_b = b_w_o.astype(bf16)

    h = x
    kv = None
    for layer in range(depth):
        if layer < n_a:
            y = _norm_mod_matmul(h, norm_mix[layer], mods, layer, 0, 1, a_w_in_b, layer, f32)
            o = _hgrn_core(y, a_lb, a_out_norm[layer], layer)
            h = _proj_residual(o, a_w_out_b, layer, h, mods, layer, 2)
        else:
            j = layer - n_a
            q = _norm_mod_matmul(h, norm_mix[layer], mods, layer, 0, 1, b_w_q_b, j, bf16,
                                 out_scale=B_HEAD_DIM ** -0.5)
            o = _attn_core(q, kv, b_rel_bias[j])
            h = _proj_residual(o, b_w_o_b, j, h, mods, layer, 2)
        h = _ffn(h, norm_ffn[layer], mods, layer, ffn_w_in_b, ffn_w_out_b, final_norm,
                 final_norm=(layer == depth - 1))
        if layer == n_a - 1:
            kv = _norm_mod_matmul(h, kv_norm, kv_mods, 0, 0, 1, kv_w_b, 0, bf16)
    return h
```

```python
import functools

import numpy as np
import jax
import jax.numpy as jnp
from jax import lax
from jax.experimental import pallas as pl
from jax.experimental.pallas import tpu as pltpu

V7X_LANES = 128
V7X_SUBLANES = 8
V7X_VMEM_LIMIT_BYTES = 56 * 1024 * 1024

CHUNK = 64
A_HEADS = 8
A_KEY_DIM = 128
B_HEADS = 16
B_HEAD_DIM = 64
B_PAST_CHUNKS = 8
REL_CLIP = 256
NORM_EPS = 1e-6
N_MOD = 6
MASK_VALUE = -1e30
MIN_FORGET = 1e-30

SUB = V7X_SUBLANES
BLOCKS_PER_CHUNK = CHUNK // SUB
PAIR = 2 * A_KEY_DIM
N_OFF_PAIRS = BLOCKS_PER_CHUNK * (BLOCKS_PER_CHUNK - 1) // 2

HGRN_TILE = 256
ATT_TILE = 256
ATT_WIN = ATT_TILE + B_PAST_CHUNKS * CHUNK
ATT_GROUP = 4
ATT_GROUP_LANES = ATT_GROUP * B_HEAD_DIM
LOG2_E = 1.4426950408889634
ROW_TILE = 512
FFN_SPLIT = 2
MOD_MAX_TILE = 2048

f32 = jnp.float32
bf16 = jnp.bfloat16


def _params(*sem):
    return pltpu.CompilerParams(dimension_semantics=sem, vmem_limit_bytes=V7X_VMEM_LIMIT_BYTES)


def _resident(shape, index_map):
    return pl.BlockSpec(shape, index_map, pipeline_mode=pl.Buffered(1))


def _sigmoid(x):
    return 1.0 / (1.0 + jnp.exp(-x))


def _norm_mod(x, gain, shift, scale):
    y = x * lax.rsqrt(jnp.mean(x * x, axis=-1, keepdims=True) + NORM_EPS)
    return (y * gain) * (1.0 + scale) + shift


def _mod_kernel(c_ref, w_ref, b_ref, o_ref):
    c = c_ref[...]
    c_act = c * _sigmoid(c)
    o_ref[...] = jnp.dot(c_act, w_ref[...], preferred_element_type=f32,
                         precision=lax.Precision.HIGHEST) + b_ref[...]


def _modulation(c, w, b):
    n_layers, d, n = w.shape
    bsz = c.shape[0]
    tn = max(t for t in range(V7X_LANES, MOD_MAX_TILE + 1, V7X_LANES) if n % t == 0)
    return pl.pallas_call(
        _mod_kernel,
        grid=(n_layers, n // tn),
        in_specs=[
            pl.BlockSpec((bsz, d), lambda l, j: (0, 0)),
            pl.BlockSpec((None, d, tn), lambda l, j: (l, 0, j)),
            pl.BlockSpec((None, 1, tn), lambda l, j: (l, 0, j)),
        ],
        out_specs=pl.BlockSpec((None, bsz, tn), lambda l, j: (l, 0, j)),
        out_shape=jax.ShapeDtypeStruct((n_layers, bsz, n), f32),
        compiler_params=_params("arbitrary", "arbitrary"),
        name="modulation",
    )(c, w, b.reshape(n_layers, 1, n))


def _nmm_kernel(h_ref, gain_ref, shift_ref, scale_ref, w_ref, o_ref, *, out_scale):
    u = _norm_mod(h_ref[...], gain_ref[...], shift_ref[...], scale_ref[...])
    y = jnp.dot(u.astype(bf16), w_ref[...], preferred_element_type=f32)
    if out_scale != 1.0:
        y = y * out_scale
    o_ref[...] = y.astype(o_ref.dtype)


def _mod_spec(layer, col, d):
    return pl.BlockSpec((None, None, 1, d), lambda b, i: (layer, b, 0, col))


def _norm_mod_matmul(h, gain, mods, layer, shift_col, scale_col, w, w_layer, out_dtype,
                     out_scale=1.0):
    bsz, seq, d = h.shape
    n = w.shape[-1]
    tm = min(ROW_TILE, seq)
    return pl.pallas_call(
        functools.partial(_nmm_kernel, out_scale=out_scale),
        grid=(bsz, seq // tm),
        in_specs=[
            pl.BlockSpec((None, tm, d), lambda b, i: (b, i, 0)),
            pl.BlockSpec((1, d), lambda b, i: (0, 0)),
            _mod_spec(layer, shift_col, d),
            _mod_spec(layer, scale_col, d),
            _resident((None, d, n), lambda b, i: (w_layer, 0, 0)),
        ],
        out_specs=pl.BlockSpec((None, tm, n), lambda b, i: (b, i, 0)),
        out_shape=jax.ShapeDtypeStruct((bsz, seq, n), out_dtype),
        compiler_params=_params("parallel", "arbitrary"),
        name="norm_mod_matmul",
    )(h, gain.reshape(1, d), mods, mods, w)


def _proj_res_kernel(x_ref, w_ref, h_ref, gate_ref, o_ref):
    y = jnp.dot(x_ref[...], w_ref[...], preferred_element_type=f32)
    o_ref[...] = h_ref[...] + gate_ref[...] * y


def _proj_residual(x, w, w_layer, h, mods, layer, gate_col):
    bsz, seq, d = h.shape
    kdim = x.shape[-1]
    tm = min(ROW_TILE, seq)
    return pl.pallas_call(
        _proj_res_kernel,
        grid=(bsz, seq // tm),
        in_specs=[
            pl.BlockSpec((None, tm, kdim), lambda b, i: (b, i, 0)),
            _resident((None, kdim, d), lambda b, i: (w_layer, 0, 0)),
            pl.BlockSpec((None, tm, d), lambda b, i: (b, i, 0)),
            _mod_spec(layer, gate_col, d),
        ],
        out_specs=pl.BlockSpec((None, tm, d), lambda b, i: (b, i, 0)),
        out_shape=jax.ShapeDtypeStruct((bsz, seq, d), f32),
        compiler_params=_params("parallel", "arbitrary"),
        name="proj_residual",
    )(x, w, h, mods)


def _ffn_kernel(h_ref, gain_ref, shift_ref, scale_ref, gate_ref, win_ref, wout_ref, fgain_ref,
                o_ref, *, ffn_dim, final_norm):
    x = h_ref[...]
    u = _norm_mod(x, gain_ref[...], shift_ref[...], scale_ref[...]).astype(bf16)
    fc = ffn_dim // FFN_SPLIT
    acc = None
    for j in range(FFN_SPLIT):
        a = jnp.dot(u, win_ref[:, j * fc:(j + 1) * fc], preferred_element_type=f32)
        b = jnp.dot(u, win_ref[:, ffn_dim + j * fc:ffn_dim + (j + 1) * fc],
                    preferred_element_type=f32)
        act = ((a * _sigmoid(a)) * b).astype(bf16)
        p = jnp.dot(act, wout_ref[j * fc:(j + 1) * fc, :], preferred_element_type=f32)
        acc = p if acc is None else acc + p
    out = x + gate_ref[...] * acc
    if final_norm:
        out = out * lax.rsqrt(jnp.mean(out * out, axis=-1, keepdims=True) + NORM_EPS)
        out = out * fgain_ref[...]
    o_ref[...] = out


def _ffn(h, gain, mods, layer, w_in, w_out, final_gain, final_norm):
    bsz, seq, d = h.shape
    ffn_dim = w_out.shape[1]
    tm = min(ROW_TILE, seq)
    return pl.pallas_call(
        functools.partial(_ffn_kernel, ffn_dim=ffn_dim, final_norm=final_norm),
        grid=(bsz, seq // tm),
        in_specs=[
            pl.BlockSpec((None, tm, d), lambda b, i: (b, i, 0)),
            pl.BlockSpec((1, d), lambda b, i: (0, 0)),
            _mod_spec(layer, 3, d),
            _mod_spec(layer, 4, d),
            _mod_spec(layer, 5, d),
            _resident((None, d, 2 * ffn_dim), lambda b, i: (layer, 0, 0)),
            _resident((None, ffn_dim, d), lambda b, i: (layer, 0, 0)),
            pl.BlockSpec((1, d), lambda b, i: (0, 0)),
        ],
        out_specs=pl.BlockSpec((None, tm, d), lambda b, i: (b, i, 0)),
        out_shape=jax.ShapeDtypeStruct((bsz, seq, d), f32),
        compiler_params=_params("parallel", "arbitrary"),
        name="ffn",
    )(h, gain.reshape(1, d), mods, mods, mods, w_in, w_out, final_gain.reshape(1, d))


def _block_diag2(a, b):
    z = jnp.zeros_like(a)
    return jnp.concatenate([jnp.concatenate([a, z], axis=1),
                            jnp.concatenate([z, b], axis=1)], axis=0)


def _block_cumsum(x3):
    sub = lax.broadcasted_iota(jnp.int32, x3.shape, 1)
    for sh in (1, 2, 4):
        x3 = x3 + jnp.where(sub >= sh, pltpu.roll(x3, sh, axis=1), 0.0)
    return x3


def _hgrn_core_kernel(y_ref, alb_ref, gain_ref, wsel_ref, ones_ref, o_ref,
                      st_ref, q_scr, w_scr, c_scr, qt_scr, kt_scr, ctmp_scr, ctot_scr, ginc_scr, lhs_scr,
                      diag_scr, o_scr, *, layer, tile):
    d = A_HEADS * A_KEY_DIM
    nb = tile // SUB
    n_chunks = tile // CHUNK
    n_pairs = A_HEADS // 2

    @pl.when(pl.program_id(1) == 0)
    def _():
        st_ref[...] = jnp.zeros_like(st_ref)

    a_lb = alb_ref[...]
    e_lb = jnp.exp(a_lb - jnp.max(a_lb, axis=0, keepdims=True))
    sm = e_lb / jnp.sum(e_lb, axis=0, keepdims=True)
    lb = jnp.zeros((1, d), f32)
    for m in range(1, layer + 1):
        lb = lb + sm[m:m + 1, :]
    oml = 1.0 - lb

    qpre = y_ref[:, 0:d]
    zf = y_ref[:, d:2 * d]
    q = qpre * _sigmoid(qpre)
    sig = _sigmoid(zf)
    log_f = jnp.log2(jnp.maximum(lb + oml * sig, MIN_FORGET))
    k_sign = jnp.where(oml < 0.0, -1.0, 1.0)
    log_k = jnp.log2(jnp.abs(oml) * (1.0 - sig))

    c3 = _block_cumsum(log_f.reshape(nb, SUB, d))
    ctot3 = c3[:, SUB - 1:SUB, :]
    w3 = c3 - log_k.reshape(nb, SUB, d)
    q_scr[...] = q * k_sign
    c_scr[...] = c3.reshape(tile, d)
    w_scr[...] = w3.reshape(tile, d)
    qt_scr[...] = (q.reshape(nb, SUB, d) * jnp.exp2(c3)).reshape(tile, d)
    kt_scr[...] = (jnp.exp2(ctot3 - w3).reshape(tile, d)) * k_sign

    for h in range(d // V7X_LANES):
        ctmp_scr[h] = c_scr[:, h * V7X_LANES:(h + 1) * V7X_LANES]
    ctot = jnp.concatenate([ctmp_scr[h, pl.ds(SUB - 1, nb, stride=SUB), :]
                            for h in range(d // V7X_LANES)], axis=1)
    ctot_scr[...] = ctot
    ginc_scr[...] = _block_cumsum(ctot.reshape(n_chunks, BLOCKS_PER_CHUNK, d)).reshape(nb, d)

    sub = lax.broadcasted_iota(jnp.int32, (nb, SUB, d), 1)
    for s in range(SUB):
        c3 = c_scr[...].reshape(nb, SUB, d)
        w_s = w_scr[...].reshape(nb, SUB, d)[:, s:s + 1, :]
        p = q_scr[...].reshape(nb, SUB, d) * jnp.exp2(jnp.where(sub >= s, c3 - w_s, MASK_VALUE))
        p = p.reshape(tile, d).astype(bf16)
        for m in range(n_pairs):
            lhs_scr[m * tile:(m + 1) * tile, s * PAIR:(s + 1) * PAIR] = p[:, m * PAIR:(m + 1) * PAIR]
    diag_scr[...] = jnp.dot(lhs_scr[...], wsel_ref[...], preferred_element_type=f32)

    lane = lax.broadcasted_iota(jnp.int32, (SUB, V7X_LANES), 1)
    row = lax.broadcasted_iota(jnp.int32, (SUB, V7X_LANES), 0)
    col_block = (lane % CHUNK) // SUB
    col_sub = lane % SUB

    def chunk_body(ci, carry):
        r0 = pl.multiple_of(ci * CHUNK, CHUNK)
        b0 = pl.multiple_of(ci * BLOCKS_PER_CHUNK, BLOCKS_PER_CHUNK)
        qt = qt_scr[pl.ds(r0, CHUNK), :]
        kt = kt_scr[pl.ds(r0, CHUNK), :]
        v = y_ref[pl.ds(r0, CHUNK), 2 * d:3 * d]
        ginc = ginc_scr[pl.ds(b0, BLOCKS_PER_CHUNK), :]
        ctot_c = ctot_scr[pl.ds(b0, BLOCKS_PER_CHUNK), :]
        gexc = ginc - ctot_c
        btot = ginc[BLOCKS_PER_CHUNK - 1:BLOCKS_PER_CHUNK, :]
        e_g = jnp.exp2(gexc)
        e_r = jnp.exp2(btot - ginc)
        qh = jnp.concatenate([qt[SUB * i:SUB * (i + 1)] * e_g[i:i + 1] for i in range(BLOCKS_PER_CHUNK)],
                             axis=0).astype(bf16)
        kh = jnp.concatenate([kt[SUB * i:SUB * (i + 1)] * e_r[i:i + 1] for i in range(BLOCKS_PER_CHUNK)],
                             axis=0).astype(bf16)
        rows = []
        for i in range(1, BLOCKS_PER_CHUNK):
            d_i = jnp.exp2(gexc[i:i + 1] - ginc[0:i])
            for j in range(i):
                rows.append(qt[SUB * i:SUB * (i + 1)] * d_i[j:j + 1])
        lhs_off = jnp.concatenate(rows, axis=0).astype(bf16)
        ktb = kt.astype(bf16)
        vb = v.astype(bf16)
        e_b = jnp.exp2(btot)

        for m in range(n_pairs):
            lo, mid, hi = m * PAIR, m * PAIR + A_KEY_DIM, (m + 1) * PAIR
            k_bd = _block_diag2(ktb[:, lo:mid], ktb[:, mid:hi])
            off = lax.dot_general(lhs_off[:, lo:hi], k_bd, (((1,), (1,)), ((), ())),
                                  preferred_element_type=f32)
            dg = diag_scr[pl.ds(pl.multiple_of(m * tile + r0, CHUNK), CHUNK), :]
            blocks = []
            for i in range(BLOCKS_PER_CHUNK):
                a = jnp.where(col_block == i, jnp.where(col_sub <= row, dg[SUB * i:SUB * (i + 1)], 0.0), 0.0)
                for j in range(i):
                    pidx = i * (i - 1) // 2 + j
                    a = jnp.where(col_block == j, off[SUB * pidx:SUB * (pidx + 1)], a)
                blocks.append(a)
            a2 = jnp.concatenate(blocks, axis=0).astype(bf16)
            v_bd = _block_diag2(vb[:, lo:mid], vb[:, mid:hi])
            o_intra = jnp.dot(a2, v_bd, preferred_element_type=f32)
            st_e = st_ref[2 * m]
            st_o = st_ref[2 * m + 1]
            st_bd = _block_diag2(st_e.astype(bf16), st_o.astype(bf16))
            o_inter = lax.dot_general(qh[:, lo:hi], st_bd, (((1,), (1,)), ((), ())),
                                      preferred_element_type=f32)
            o_scr[pl.ds(r0, CHUNK), lo:hi] = o_intra + o_inter
            upd = lax.dot_general(vb[:, lo:hi], kh[:, lo:hi], (((0,), (0,)), ((), ())),
                                  preferred_element_type=f32)
            st_ref[2 * m] = e_b[:, lo:mid] * st_e + upd[:A_KEY_DIM, :A_KEY_DIM]
            st_ref[2 * m + 1] = e_b[:, mid:hi] * st_o + upd[A_KEY_DIM:, A_KEY_DIM:]
        return carry

    lax.fori_loop(0, n_chunks, chunk_body, 0, unroll=True)

    o = o_scr[...]
    osq = (o * o).astype(bf16)
    ms = jnp.concatenate(
        [jnp.dot(osq[:, m * PAIR:(m + 1) * PAIR], ones_ref[...], preferred_element_type=f32)
         for m in range(n_pairs)], axis=1) * (1.0 / A_KEY_DIM)
    g = y_ref[:, 3 * d:4 * d]
    out = o * lax.rsqrt(ms + NORM_EPS) * gain_ref[...]
    o_ref[...] = (out * (g * _sigmoid(g))).astype(o_ref.dtype)


def _hgrn_selector():
    r = np.arange(SUB * PAIR)
    s_local, h_r = r // PAIR, (r % PAIR) // A_KEY_DIM
    c = np.arange(2 * CHUNK)
    h_c, s_c = c // CHUNK, c % CHUNK
    sel = (h_r[:, None] == h_c[None, :]) & (s_local[:, None] == (s_c % SUB)[None, :])
    return jnp.asarray(sel, dtype=bf16)


def _hgrn_core(y, a_lb, o_gain, layer):
    bsz, seq, _ = y.shape
    d = A_HEADS * A_KEY_DIM
    tile = min(HGRN_TILE, seq)
    n_pairs = A_HEADS // 2
    ones_bd = jnp.asarray(np.kron(np.eye(2), np.ones((A_KEY_DIM, A_KEY_DIM))), dtype=bf16)
    gain_t = jnp.tile(o_gain.astype(f32), A_HEADS).reshape(1, d)
    n_a = a_lb.shape[0]
    return pl.pallas_call(
        functools.partial(_hgrn_core_kernel, layer=layer, tile=tile),
        grid=(bsz, seq // tile),
        in_specs=[
            pl.BlockSpec((None, tile, 4 * d), lambda b, i: (b, i, 0)),
            pl.BlockSpec((n_a, d), lambda b, i: (0, 0)),
            pl.BlockSpec((1, d), lambda b, i: (0, 0)),
            _resident((SUB * PAIR, 2 * CHUNK), lambda b, i: (0, 0)),
            _resident((PAIR, PAIR), lambda b, i: (0, 0)),
        ],
        out_specs=pl.BlockSpec((None, tile, d), lambda b, i: (b, i, 0)),
        out_shape=jax.ShapeDtypeStruct((bsz, seq, d), bf16),
        scratch_shapes=[
            pltpu.VMEM((A_HEADS, A_KEY_DIM, A_KEY_DIM), f32),
            pltpu.VMEM((tile, d), f32),
            pltpu.VMEM((tile, d), f32),
            pltpu.VMEM((tile, d), f32),
            pltpu.VMEM((tile, d), f32),
            pltpu.VMEM((tile, d), f32),
            pltpu.VMEM((d // V7X_LANES, tile, V7X_LANES), f32),
            pltpu.VMEM((tile // SUB, d), f32),
            pltpu.VMEM((tile // SUB, d), f32),
            pltpu.VMEM((n_pairs * tile, SUB * PAIR), bf16),
            pltpu.VMEM((n_pairs * tile, 2 * CHUNK), f32),
            pltpu.VMEM((tile, d), f32),
        ],
        compiler_params=_params("parallel", "arbitrary"),
        name="hgrn_core",
    )(y, a_lb.astype(f32), gain_t, _hgrn_selector(), ones_bd)


def _attn_kernel(q_ref, k0_ref, k1_ref, k2_ref, v0_ref, v1_ref, v2_ref, bias_ref, o_ref, p_scr):
    i = pl.program_id(1)
    lane = lax.broadcasted_iota(jnp.int32, (1, ATT_GROUP_LANES), 1)
    head_of_lane = lane // B_HEAD_DIM
    kcol = lax.broadcasted_iota(jnp.int32, (1, ATT_WIN), 1)
    pad_row = jnp.where(kcol >= (2 - i) * ATT_TILE, 0.0, MASK_VALUE)
    for g in range(B_HEADS // ATT_GROUP):
        lanes = slice(g * ATT_GROUP_LANES, (g + 1) * ATT_GROUP_LANES)
        q4 = q_ref[:, lanes]
        kwin = jnp.concatenate([k0_ref[:, lanes], k1_ref[:, lanes], k2_ref[:, lanes]], axis=0)
        vwin = jnp.concatenate([v0_ref[:, lanes], v1_ref[:, lanes], v2_ref[:, lanes]], axis=0)
        inv = None
        v_rows = []
        for hh in range(ATT_GROUP):
            mine = head_of_lane == hh
            qh = jnp.where(mine, q4, jnp.zeros_like(q4))
            s = lax.dot_general(qh, kwin, (((1,), (1,)), ((), ())), preferred_element_type=f32)
            s = s + bias_ref[g * ATT_GROUP + hh] + pad_row
            p = jnp.exp2(s - jnp.max(s, axis=-1, keepdims=True))
            r = 1.0 / jnp.sum(p, axis=-1, keepdims=True)
            inv = jnp.where(mine, r, 0.0) if inv is None else jnp.where(mine, r, inv)
            p_scr[:, hh * ATT_WIN:(hh + 1) * ATT_WIN] = p.astype(bf16)
            v_rows.append(jnp.where(mine, vwin, jnp.zeros_like(vwin)))
        o4 = jnp.dot(p_scr[...], jnp.concatenate(v_rows, axis=0), preferred_element_type=f32)
        o_ref[:, lanes] = (o4 * inv).astype(o_ref.dtype)


def _attn_bias_table(rel_bias):
    n_rel, n_heads = rel_bias.shape
    span = ATT_TILE + ATT_WIN
    j = np.arange(span)
    delta = np.where(j < ATT_WIN, j, j - span)
    idx = np.clip(delta - B_PAST_CHUNKS * CHUNK, -REL_CLIP, CHUNK - 1) + REL_CLIP
    e = (rel_bias.astype(f32) * LOG2_E)[jnp.asarray(idx)].T
    flat = jnp.tile(e, (1, ATT_TILE))[:, :ATT_TILE * (span - 1)]
    table = flat.reshape(n_heads, ATT_TILE, span - 1)[:, :, :ATT_WIN]
    qchunk = np.arange(ATT_TILE)[:, None] // CHUNK + B_PAST_CHUNKS
    back = qchunk - np.arange(ATT_WIN)[None, :] // CHUNK
    in_band = (back >= 0) & (back <= B_PAST_CHUNKS)
    return jnp.where(jnp.asarray(in_band)[None], table, MASK_VALUE)


def _attn_core(q, kv, rel_bias):
    bsz, seq, d = q.shape
    assert seq % ATT_TILE == 0
    bias = _attn_bias_table(rel_bias)

    def kv_spec(back, col):
        return pl.BlockSpec((None, ATT_TILE, d), lambda b, i: (b, jnp.maximum(i - back, 0), col))

    return pl.pallas_call(
        _attn_kernel,
        grid=(bsz, seq // ATT_TILE),
        in_specs=[
            pl.BlockSpec((None, ATT_TILE, d), lambda b, i: (b, i, 0)),
            kv_spec(2, 0), kv_spec(1, 0), kv_spec(0, 0),
            kv_spec(2, 1), kv_spec(1, 1), kv_spec(0, 1),
            _resident((B_HEADS, ATT_TILE, ATT_WIN), lambda b, i: (0, 0, 0)),
        ],
        out_specs=pl.BlockSpec((None, ATT_TILE, d), lambda b, i: (b, i, 0)),
        out_shape=jax.ShapeDtypeStruct((bsz, seq, d), bf16),
        scratch_shapes=[pltpu.VMEM((ATT_TILE, ATT_GROUP * ATT_WIN), bf16)],
        compiler_params=_params("parallel", "arbitrary"),
        name="band_attention",
    )(q, kv, kv, kv, kv, kv, kv, bias)


def kernel(x, c, mod_w, mod_b, norm_mix, norm_ffn, ffn_w_in, ffn_w_out, a_w_in, a_w_out, a_lb,
           a_out_norm, kv_norm, kv_mod_w, kv_mod_b, kv_w, b_w_q, b_w_o, b_rel_bias, final_norm):
    depth = mod_w.shape[0]
    n_a = a_w_in.shape[0]
    bsz = x.shape[0]

    mods = _modulation(c, mod_w, mod_b).reshape(depth, bsz, 1, -1)
    kv_mods = _modulation(c, kv_mod_w[None], kv_mod_b[None]).reshape(1, bsz, 1, -1)

    ffn_w_in_b = ffn_w_in.astype(bf16)
    ffn_w_out_b = ffn_w_out.astype(bf16)
    a_w_in_b = a_w_in.astype(bf16)
    a_w_out_b = a_w_out.astype(bf16)
    kv_w_b = kv_w.astype(bf16)[None]
    b_w_q_b = b_w_q.astype(bf16)
    b_w_o_b = b_w_o.astype(bf16)

    h = x
    kv = None
    for layer in range(depth):
        if layer < n_a:
            y = _norm_mod_matmul(h, norm_mix[layer], mods, layer, 0, 1, a_w_in_b, layer, f32)
            o = _hgrn_core(y, a_lb, a_out_norm[layer], layer)
            h = _proj_residual(o, a_w_out_b, layer, h, mods, layer, 2)
        else:
            j = layer - n_a
            q = _norm_mod_matmul(h, norm_mix[layer], mods, layer, 0, 1, b_w_q_b, j, bf16,
                                 out_scale=B_HEAD_DIM ** -0.5 * LOG2_E)
            o = _attn_core(q, kv, b_rel_bias[j])
            h = _proj_residual(o, b_w_o_b, j, h, mods, layer, 2)
        h = _ffn(h, norm_ffn[layer], mods, layer, ffn_w_in_b, ffn_w_out_b, final_norm,
                 final_norm=(layer == depth - 1))
        if layer == n_a - 1:
            kv = _norm_mod_matmul(h, kv_norm, kv_mods, 0, 0, 1, kv_w_b, 0, bf16)
    return h
```

```python
import functools

import numpy as np
import jax
import jax.numpy as jnp
from jax import lax
from jax.experimental import pallas as pl
from jax.experimental.pallas import tpu as pltpu

V7X_LANES = 128
V7X_SUBLANES = 8
V7X_MXU_DIM = 256
V7X_VMEM_LIMIT_BYTES = 56 * 1024 * 1024

CHUNK = 64
A_HEADS = 8
A_KEY_DIM = 128
B_HEADS = 16
B_HEAD_DIM = 64
B_PAST_CHUNKS = 8
REL_CLIP = 256
NORM_EPS = 1e-6
N_MOD = 6
MASK_VALUE = -1e30
MIN_FORGET = 1e-30
LOG2_E = 1.4426950408889634

SUB = V7X_SUBLANES
BLOCKS_PER_CHUNK = CHUNK // SUB
PAIR = 2 * A_KEY_DIM
N_OFF_PAIRS = BLOCKS_PER_CHUNK * (BLOCKS_PER_CHUNK - 1) // 2

HGRN_TILE = 256
ATT_TILE = 256
ATT_WIN = ATT_TILE + B_PAST_CHUNKS * CHUNK
ATT_GROUP = 4
ATT_GROUP_LANES = ATT_GROUP * B_HEAD_DIM
ROW_TILE = 512
MOD_MAX_TILE = 2048

f32 = jnp.float32
bf16 = jnp.bfloat16


def _params(*sem):
    return pltpu.CompilerParams(dimension_semantics=sem, vmem_limit_bytes=V7X_VMEM_LIMIT_BYTES)


def _resident(shape, index_map):
    return pl.BlockSpec(shape, index_map, pipeline_mode=pl.Buffered(1))


def _row_spec(d):
    return pl.BlockSpec((1, d), lambda b, i: (0, 0))


def _mod_spec(layer, col, d):
    return pl.BlockSpec((None, None, 1, d), lambda b, i: (layer, b, 0, col))


def _sigmoid(x):
    return 1.0 / (1.0 + jnp.exp(-x))


def _norm_mod(x, gain, shift, scale):
    y = x * lax.rsqrt(jnp.mean(x * x, axis=-1, keepdims=True) + NORM_EPS)
    return (y * gain) * (1.0 + scale) + shift


def _mod_kernel(c_ref, w_ref, b_ref, o_ref):
    c = c_ref[...]
    c_act = c * _sigmoid(c)
    o_ref[...] = jnp.dot(c_act, w_ref[...], preferred_element_type=f32,
                         precision=lax.Precision.HIGHEST) + b_ref[...]


def _modulation(c, w, b):
    n_layers, d, n = w.shape
    bsz = c.shape[0]
    tn = max(t for t in range(V7X_LANES, MOD_MAX_TILE + 1, V7X_LANES) if n % t == 0)
    return pl.pallas_call(
        _mod_kernel,
        grid=(n_layers, n // tn),
        in_specs=[
            pl.BlockSpec((bsz, d), lambda l, j: (0, 0)),
            pl.BlockSpec((None, d, tn), lambda l, j: (l, 0, j)),
            pl.BlockSpec((None, 1, tn), lambda l, j: (l, 0, j)),
        ],
        out_specs=pl.BlockSpec((None, bsz, tn), lambda l, j: (l, 0, j)),
        out_shape=jax.ShapeDtypeStruct((n_layers, bsz, n), f32),
        compiler_params=_params("arbitrary", "arbitrary"),
        name="modulation",
    )(c, w, b.reshape(n_layers, 1, n))


def _ffn_kernel(*refs, ffn_dim, final_norm, with_kv):
    h_ref, gain_ref, shift_ref, scale_ref, gate_ref, win_ref, wout_ref = refs[:7]
    rest = refs[7:]
    x = h_ref[...]
    u = _norm_mod(x, gain_ref[...], shift_ref[...], scale_ref[...]).astype(bf16)
    mid = (ffn_dim // 2) // V7X_MXU_DIM * V7X_MXU_DIM
    acc = None
    for lo, hi in ((0, mid), (mid, ffn_dim)):
        a = jnp.dot(u, win_ref[:, lo:hi], preferred_element_type=f32)
        b = jnp.dot(u, win_ref[:, ffn_dim + lo:ffn_dim + hi], preferred_element_type=f32)
        act = ((a * _sigmoid(a)) * b).astype(bf16)
        p = jnp.dot(act, wout_ref[lo:hi, :], preferred_element_type=f32)
        acc = p if acc is None else acc + p
    out = x + gate_ref[...] * acc
    if with_kv:
        kgain_ref, kshift_ref, kscale_ref, kvw_ref, o_ref, kv_ref = rest
        ukv = _norm_mod(out, kgain_ref[...], kshift_ref[...], kscale_ref[...]).astype(bf16)
        kv_ref[...] = jnp.dot(ukv, kvw_ref[...], preferred_element_type=f32).astype(kv_ref.dtype)
    elif final_norm:
        fgain_ref, o_ref = rest
        out = out * lax.rsqrt(jnp.mean(out * out, axis=-1, keepdims=True) + NORM_EPS)
        out = out * fgain_ref[...]
    else:
        (o_ref,) = rest
    o_ref[...] = out


def _ffn(h, gain, mods, layer, w_in, w_out, final_gain=None, kv=None):
    bsz, seq, d = h.shape
    ffn_dim = w_out.shape[1]
    tm = min(ROW_TILE, seq)
    h_spec = pl.BlockSpec((None, tm, d), lambda b, i: (b, i, 0))
    in_specs = [
        h_spec, _row_spec(d), _mod_spec(layer, 3, d), _mod_spec(layer, 4, d), _mod_spec(layer, 5, d),
        _resident((None, d, 2 * ffn_dim), lambda b, i: (layer, 0, 0)),
        _resident((None, ffn_dim, d), lambda b, i: (layer, 0, 0)),
    ]
    args = [h, gain.reshape(1, d), mods, mods, mods, w_in, w_out]
    out_specs = h_spec
    out_shape = jax.ShapeDtypeStruct((bsz, seq, d), f32)
    if kv is not None:
        kv_norm, kv_mods, kv_w = kv
        n_kv = kv_w.shape[-1]
        in_specs += [_row_spec(d), _mod_spec(0, 0, d), _mod_spec(0, 1, d),
                     _resident((d, n_kv), lambda b, i: (0, 0))]
        args += [kv_norm.reshape(1, d), kv_mods, kv_mods, kv_w]
        out_specs = (h_spec, pl.BlockSpec((None, tm, n_kv), lambda b, i: (b, i, 0)))
        out_shape = (out_shape, jax.ShapeDtypeStruct((bsz, seq, n_kv), bf16))
    elif final_gain is not None:
        in_specs += [_row_spec(d)]
        args += [final_gain.reshape(1, d)]
    return pl.pallas_call(
        functools.partial(_ffn_kernel, ffn_dim=ffn_dim, final_norm=final_gain is not None,
                          with_kv=kv is not None),
        grid=(bsz, seq // tm),
        in_specs=in_specs,
        out_specs=out_specs,
        out_shape=out_shape,
        compiler_params=_params("parallel", "arbitrary"),
        name="ffn",
    )(*args)


def _block_diag2(a, b):
    z = jnp.zeros_like(a)
    return jnp.concatenate([jnp.concatenate([a, z], axis=1),
                            jnp.concatenate([z, b], axis=1)], axis=0)


def _block_cumsum(x3):
    sub = lax.broadcasted_iota(jnp.int32, x3.shape, 1)
    for sh in (1, 2, 4):
        x3 = x3 + jnp.where(sub >= sh, pltpu.roll(x3, sh, axis=1), 0.0)
    return x3


def _hgrn_layer_kernel(h_ref, hnext_ref, gain_ref, shift_ref, scale_ref, gate_ref, win_ref, wout_ref,
                       alb_ref, ogain_ref, wsel_ref, wselw_ref, ones_ref, o_ref,
                       st_ref, y_scr, q_scr, w_scr, c_scr, qt_scr, kt_scr, vb_scr, sg_scr, ctmp_scr,
                       ctot_scr, ginc_scr, lhs_scr, diag_scr, o_scr, *, layer, tile):
    d = A_HEADS * A_KEY_DIM
    nb = tile // SUB
    n_chunks = tile // CHUNK
    n_pairs = A_HEADS // 2

    def in_proj(src_ref):
        u = _norm_mod(src_ref[...], gain_ref[...], shift_ref[...], scale_ref[...]).astype(bf16)
        y_scr[...] = jnp.dot(u, win_ref[...], preferred_element_type=f32)

    @pl.when(pl.program_id(1) == 0)
    def _():
        st_ref[...] = jnp.zeros_like(st_ref)
        in_proj(h_ref)

    a_lb = alb_ref[...]
    e_lb = jnp.exp(a_lb - jnp.max(a_lb, axis=0, keepdims=True))
    sm = e_lb / jnp.sum(e_lb, axis=0, keepdims=True)
    lb = jnp.zeros((1, d), f32)
    for m in range(1, layer + 1):
        lb = lb + sm[m:m + 1, :]
    oml = 1.0 - lb

    qpre = y_scr[:, 0:d]
    zf = y_scr[:, d:2 * d]
    q = qpre * _sigmoid(qpre)
    sig = _sigmoid(zf)
    log_f = jnp.log2(jnp.maximum(lb + oml * sig, MIN_FORGET))
    k_sign = jnp.where(oml < 0.0, -1.0, 1.0)
    log_k = jnp.log2(jnp.abs(oml) * (1.0 - sig))

    c3 = _block_cumsum(log_f.reshape(nb, SUB, d))
    ctot3 = c3[:, SUB - 1:SUB, :]
    w3 = c3 - log_k.reshape(nb, SUB, d)
    q_scr[...] = q * k_sign
    c_scr[...] = c3.reshape(tile, d)
    w_scr[...] = w3.reshape(tile, d)
    qt_scr[...] = (q.reshape(nb, SUB, d) * jnp.exp2(c3)).reshape(tile, d)
    kt_scr[...] = (jnp.exp2(ctot3 - w3).reshape(tile, d)) * k_sign
    vb_scr[...] = y_scr[:, 2 * d:3 * d].astype(bf16)
    g = y_scr[:, 3 * d:4 * d]
    sg_scr[...] = g * _sigmoid(g)
    in_proj(hnext_ref)

    for h in range(d // V7X_LANES):
        ctmp_scr[h] = c_scr[:, h * V7X_LANES:(h + 1) * V7X_LANES]
    ctot = jnp.concatenate([ctmp_scr[h, pl.ds(SUB - 1, nb, stride=SUB), :]
                            for h in range(d // V7X_LANES)], axis=1)
    ctot_scr[...] = ctot
    ginc_scr[...] = _block_cumsum(ctot.reshape(n_chunks, BLOCKS_PER_CHUNK, d)).reshape(nb, d)

    sub = lax.broadcasted_iota(jnp.int32, (nb, SUB, d), 1)
    for s in range(SUB):
        c3 = c_scr[...].reshape(nb, SUB, d)
        w_s = w_scr[...].reshape(nb, SUB, d)[:, s:s + 1, :]
        p = q_scr[...].reshape(nb, SUB, d) * jnp.exp2(jnp.where(sub >= s, c3 - w_s, MASK_VALUE))
        p = p.reshape(tile, d).astype(bf16)
        for m in range(n_pairs):
            lhs_scr[m * tile:(m + 1) * tile, s * PAIR:(s + 1) * PAIR] = p[:, m * PAIR:(m + 1) * PAIR]
    half = n_pairs * tile // 2
    diag_scr[0:half] = jnp.dot(lhs_scr[0:half], wsel_ref[...], preferred_element_type=f32)
    diag_scr[half:] = jnp.dot(lhs_scr[half:], wselw_ref[...],
                              preferred_element_type=f32)[:, :2 * CHUNK]

    lane = lax.broadcasted_iota(jnp.int32, (SUB, V7X_LANES), 1)
    row = lax.broadcasted_iota(jnp.int32, (SUB, V7X_LANES), 0)
    col_block = (lane % CHUNK) // SUB
    col_sub = lane % SUB

    def chunk_body(ci, carry):
        r0 = pl.multiple_of(ci * CHUNK, CHUNK)
        b0 = pl.multiple_of(ci * BLOCKS_PER_CHUNK, BLOCKS_PER_CHUNK)
        qt = qt_scr[pl.ds(r0, CHUNK), :]
        kt = kt_scr[pl.ds(r0, CHUNK), :]
        vb = vb_scr[pl.ds(r0, CHUNK), :]
        ginc = ginc_scr[pl.ds(b0, BLOCKS_PER_CHUNK), :]
        ctot_c = ctot_scr[pl.ds(b0, BLOCKS_PER_CHUNK), :]
        gexc = ginc - ctot_c
        btot = ginc[BLOCKS_PER_CHUNK - 1:BLOCKS_PER_CHUNK, :]
        e_g = jnp.exp2(gexc)
        e_r = jnp.exp2(btot - ginc)
        qh = jnp.concatenate([qt[SUB * i:SUB * (i + 1)] * e_g[i:i + 1] for i in range(BLOCKS_PER_CHUNK)],
                             axis=0).astype(bf16)
        kh = jnp.concatenate([kt[SUB * i:SUB * (i + 1)] * e_r[i:i + 1] for i in range(BLOCKS_PER_CHUNK)],
                             axis=0).astype(bf16)
        rows = []
        for i in range(1, BLOCKS_PER_CHUNK):
            d_i = jnp.exp2(gexc[i:i + 1] - ginc[0:i])
            for j in range(i):
                rows.append(qt[SUB * i:SUB * (i + 1)] * d_i[j:j + 1])
        lhs_off = jnp.concatenate(rows, axis=0).astype(bf16)
        ktb = kt.astype(bf16)
        e_b = jnp.exp2(btot)

        for m in range(n_pairs):
            lo, mid, hi = m * PAIR, m * PAIR + A_KEY_DIM, (m + 1) * PAIR
            k_bd = _block_diag2(ktb[:, lo:mid], ktb[:, mid:hi])
            off = lax.dot_general(lhs_off[:, lo:hi], k_bd, (((1,), (1,)), ((), ())),
                                  preferred_element_type=f32)
            dg = diag_scr[pl.ds(pl.multiple_of(m * tile + r0, CHUNK), CHUNK), :]
            blocks = []
            for i in range(BLOCKS_PER_CHUNK):
                a = jnp.where(col_block == i, jnp.where(col_sub <= row, dg[SUB * i:SUB * (i + 1)], 0.0), 0.0)
                for j in range(i):
                    pidx = i * (i - 1) // 2 + j
                    a = jnp.where(col_block == j, off[SUB * pidx:SUB * (pidx + 1)], a)
                blocks.append(a)
            a2 = jnp.concatenate(blocks, axis=0).astype(bf16)
            v_bd = _block_diag2(vb[:, lo:mid], vb[:, mid:hi])
            o_intra = jnp.dot(a2, v_bd, preferred_element_type=f32)
            st_e = st_ref[2 * m]
            st_o = st_ref[2 * m + 1]
            st_bd = _block_diag2(st_e.astype(bf16), st_o.astype(bf16))
            o_inter = lax.dot_general(qh[:, lo:hi], st_bd, (((1,), (1,)), ((), ())),
                                      preferred_element_type=f32)
            o_scr[pl.ds(r0, CHUNK), lo:hi] = o_intra + o_inter
            upd = lax.dot_general(vb[:, lo:hi], kh[:, lo:hi], (((0,), (0,)), ((), ())),
                                  preferred_element_type=f32)
            st_ref[2 * m] = e_b[:, lo:mid] * st_e + upd[:A_KEY_DIM, :A_KEY_DIM]
            st_ref[2 * m + 1] = e_b[:, mid:hi] * st_o + upd[A_KEY_DIM:, A_KEY_DIM:]
        return carry

    lax.fori_loop(0, n_chunks, chunk_body, 0, unroll=True)

    o = o_scr[...]
    osq = (o * o).astype(bf16)
    ms = jnp.concatenate(
        [jnp.dot(osq[:, m * PAIR:(m + 1) * PAIR], ones_ref[...], preferred_element_type=f32)
         for m in range(n_pairs)], axis=1) * (1.0 / A_KEY_DIM)
    out = o * lax.rsqrt(ms + NORM_EPS) * ogain_ref[...]
    out = (out * sg_scr[...]).astype(bf16)
    mix = jnp.dot(out, wout_ref[...], preferred_element_type=f32)
    o_ref[...] = h_ref[...] + gate_ref[...] * mix


def _hgrn_selector():
    r = np.arange(SUB * PAIR)
    s_local, h_r = r // PAIR, (r % PAIR) // A_KEY_DIM
    c = np.arange(2 * CHUNK)
    h_c, s_c = c // CHUNK, c % CHUNK
    sel = (h_r[:, None] == h_c[None, :]) & (s_local[:, None] == (s_c % SUB)[None, :])
    return jnp.asarray(sel, dtype=bf16)


def _hgrn_layer(h, gain, mods, layer, w_in, w_out, a_lb, o_gain):
    bsz, seq, d = h.shape
    assert d == A_HEADS * A_KEY_DIM
    tile = min(HGRN_TILE, seq)
    n_pairs = A_HEADS // 2
    n_in = w_in.shape[-1]
    ones_bd = jnp.asarray(np.kron(np.eye(2), np.ones((A_KEY_DIM, A_KEY_DIM))), dtype=bf16)
    gain_t = jnp.tile(o_gain.astype(f32), A_HEADS).reshape(1, d)
    n_a = a_lb.shape[0]
    sel = _hgrn_selector()
    h_spec = pl.BlockSpec((None, tile, d), lambda b, i: (b, i, 0))
    return pl.pallas_call(
        functools.partial(_hgrn_layer_kernel, layer=layer, tile=tile),
        grid=(bsz, seq // tile),
        in_specs=[
            h_spec,
            pl.BlockSpec((None, tile, d), lambda b, i: (b, jnp.minimum(i + 1, seq // tile - 1), 0)),
            _row_spec(d), _mod_spec(layer, 0, d), _mod_spec(layer, 1, d),
            _mod_spec(layer, 2, d),
            _resident((None, d, n_in), lambda b, i: (layer, 0, 0)),
            _resident((None, d, d), lambda b, i: (layer, 0, 0)),
            pl.BlockSpec((n_a, d), lambda b, i: (0, 0)),
            _row_spec(d),
            _resident((SUB * PAIR, 2 * CHUNK), lambda b, i: (0, 0)),
            _resident((SUB * PAIR, 4 * CHUNK), lambda b, i: (0, 0)),
            _resident((PAIR, PAIR), lambda b, i: (0, 0)),
        ],
        out_specs=h_spec,
        out_shape=jax.ShapeDtypeStruct((bsz, seq, d), f32),
        scratch_shapes=[
            pltpu.VMEM((A_HEADS, A_KEY_DIM, A_KEY_DIM), f32),
            pltpu.VMEM((tile, n_in), f32),
            pltpu.VMEM((tile, d), f32),
            pltpu.VMEM((tile, d), f32),
            pltpu.VMEM((tile, d), f32),
            pltpu.VMEM((tile, d), f32),
            pltpu.VMEM((tile, d), f32),
            pltpu.VMEM((tile, d), bf16),
            pltpu.VMEM((tile, d), f32),
            pltpu.VMEM((d // V7X_LANES, tile, V7X_LANES), f32),
            pltpu.VMEM((tile // SUB, d), f32),
            pltpu.VMEM((tile // SUB, d), f32),
            pltpu.VMEM((n_pairs * tile, SUB * PAIR), bf16),
            pltpu.VMEM((n_pairs * tile, 2 * CHUNK), f32),
            pltpu.VMEM((tile, d), f32),
        ],
        compiler_params=_params("parallel", "arbitrary"),
        name="hgrn_layer",
    )(h, h, gain.reshape(1, d), mods, mods, mods, w_in, w_out, a_lb.astype(f32), gain_t,
      sel, jnp.concatenate([sel, jnp.zeros_like(sel)], axis=1), ones_bd)


def _attn_layer_kernel(h_ref, hnext_ref, gain_ref, shift_ref, scale_ref, gate_ref, wq_ref, wo_ref,
                       k0_ref, k1_ref, k2_ref, v0_ref, v1_ref, v2_ref, bias_ref, o_ref,
                       q_scr, qnext_scr, p_scr, oa_scr):
    i = pl.program_id(1)

    def q_proj(src_ref):
        u = _norm_mod(src_ref[...], gain_ref[...], shift_ref[...], scale_ref[...]).astype(bf16)
        q = jnp.dot(u, wq_ref[...], preferred_element_type=f32) * (B_HEAD_DIM ** -0.5 * LOG2_E)
        qnext_scr[...] = q.astype(bf16)

    @pl.when(i == 0)
    def _():
        q_proj(h_ref)

    q_scr[...] = qnext_scr[...]
    q_proj(hnext_ref)

    lane = lax.broadcasted_iota(jnp.int32, (1, ATT_GROUP_LANES), 1)
    head_of_lane = lane // B_HEAD_DIM
    kcol = lax.broadcasted_iota(jnp.int32, (1, ATT_WIN), 1)
    pad_row = jnp.where(kcol >= (2 - i) * ATT_TILE, 0.0, MASK_VALUE)
    for g in range(B_HEADS // ATT_GROUP):
        lanes = slice(g * ATT_GROUP_LANES, (g + 1) * ATT_GROUP_LANES)
        q4 = q_scr[:, lanes]
        kwin = jnp.concatenate([k0_ref[:, lanes], k1_ref[:, lanes], k2_ref[:, lanes]], axis=0)
        vwin = jnp.concatenate([v0_ref[:, lanes], v1_ref[:, lanes], v2_ref[:, lanes]], axis=0)
        inv = None
        v_rows = []
        for hh in range(ATT_GROUP):
            mine = head_of_lane == hh
            qh = jnp.where(mine, q4, jnp.zeros_like(q4))
            s = lax.dot_general(qh, kwin, (((1,), (1,)), ((), ())), preferred_element_type=f32)
            s = s + bias_ref[g * ATT_GROUP + hh] + pad_row
            p = jnp.exp2(s - jnp.max(s, axis=-1, keepdims=True))
            r = 1.0 / jnp.sum(p, axis=-1, keepdims=True)
            inv = jnp.where(mine, r, 0.0) if inv is None else jnp.where(mine, r, inv)
            p_scr[:, hh * ATT_WIN:(hh + 1) * ATT_WIN] = p.astype(bf16)
            v_rows.append(jnp.where(mine, vwin, jnp.zeros_like(vwin)))
        o4 = jnp.dot(p_scr[...], jnp.concatenate(v_rows, axis=0), preferred_element_type=f32)
        oa_scr[:, lanes] = (o4 * inv).astype(bf16)
    mix = jnp.dot(oa_scr[...], wo_ref[...], preferred_element_type=f32)
    o_ref[...] = h_ref[...] + gate_ref[...] * mix


def _attn_bias_table(rel_bias):
    n_rel, n_heads = rel_bias.shape
    span = ATT_TILE + ATT_WIN
    j = np.arange(span)
    delta = np.where(j < ATT_WIN, j, j - span)
    idx = np.clip(delta - B_PAST_CHUNKS * CHUNK, -REL_CLIP, CHUNK - 1) + REL_CLIP
    e = (rel_bias.astype(f32) * LOG2_E)[jnp.asarray(idx)].T
    flat = jnp.tile(e, (1, ATT_TILE))[:, :ATT_TILE * (span - 1)]
    table = flat.reshape(n_heads, ATT_TILE, span - 1)[:, :, :ATT_WIN]
    qchunk = np.arange(ATT_TILE)[:, None] // CHUNK + B_PAST_CHUNKS
    back = qchunk - np.arange(ATT_WIN)[None, :] // CHUNK
    in_band = (back >= 0) & (back <= B_PAST_CHUNKS)
    return jnp.where(jnp.asarray(in_band)[None], table, MASK_VALUE)


def _attn_layer(h, gain, mods, layer, w_q, w_o, w_layer, kv, rel_bias):
    bsz, seq, d = h.shape
    assert seq % ATT_TILE == 0 and d == B_HEADS * B_HEAD_DIM
    bias = _attn_bias_table(rel_bias)
    h_spec = pl.BlockSpec((None, ATT_TILE, d), lambda b, i: (b, i, 0))

    def kv_spec(back, col):
        return pl.BlockSpec((None, ATT_TILE, d), lambda b, i: (b, jnp.maximum(i - back, 0), col))

    return pl.pallas_call(
        _attn_layer_kernel,
        grid=(bsz, seq // ATT_TILE),
        in_specs=[
            h_spec,
            pl.BlockSpec((None, ATT_TILE, d), lambda b, i: (b, jnp.minimum(i + 1, seq // ATT_TILE - 1), 0)),
            _row_spec(d), _mod_spec(layer, 0, d), _mod_spec(layer, 1, d),
            _mod_spec(layer, 2, d),
            _resident((None, d, d), lambda b, i: (w_layer, 0, 0)),
            _resident((None, d, d), lambda b, i: (w_layer, 0, 0)),
            kv_spec(2, 0), kv_spec(1, 0), kv_spec(0, 0),
            kv_spec(2, 1), kv_spec(1, 1), kv_spec(0, 1),
            _resident((B_HEADS, ATT_TILE, ATT_WIN), lambda b, i: (0, 0, 0)),
        ],
        out_specs=h_spec,
        out_shape=jax.ShapeDtypeStruct((bsz, seq, d), f32),
        scratch_shapes=[
            pltpu.VMEM((ATT_TILE, d), bf16),
            pltpu.VMEM((ATT_TILE, d), bf16),
            pltpu.VMEM((ATT_TILE, ATT_GROUP * ATT_WIN), bf16),
            pltpu.VMEM((ATT_TILE, d), bf16),
        ],
        compiler_params=_params("parallel", "arbitrary"),
        name="attn_layer",
    )(h, h, gain.reshape(1, d), mods, mods, mods, w_q, w_o, kv, kv, kv, kv, kv, kv, bias)


def kernel(x, c, mod_w, mod_b, norm_mix, norm_ffn, ffn_w_in, ffn_w_out, a_w_in, a_w_out, a_lb,
           a_out_norm, kv_norm, kv_mod_w, kv_mod_b, kv_w, b_w_q, b_w_o, b_rel_bias, final_norm):
    depth = mod_w.shape[0]
    n_a = a_w_in.shape[0]
    bsz = x.shape[0]

    mods = _modulation(c, mod_w, mod_b).reshape(depth, bsz, 1, -1)
    kv_mods = _modulation(c, kv_mod_w[None], kv_mod_b[None]).reshape(1, bsz, 1, -1)

    ffn_w_in_b = ffn_w_in.astype(bf16)
    ffn_w_out_b = ffn_w_out.astype(bf16)
    a_w_in_b = a_w_in.astype(bf16)
    a_w_out_b = a_w_out.astype(bf16)
    kv_w_b = kv_w.astype(bf16)
    b_w_q_b = b_w_q.astype(bf16)
    b_w_o_b = b_w_o.astype(bf16)

    h = x
    kv = None
    for layer in range(depth):
        if layer < n_a:
            h = _hgrn_layer(h, norm_mix[layer], mods, layer, a_w_in_b, a_w_out_b, a_lb,
                            a_out_norm[layer])
        else:
            j = layer - n_a
            h = _attn_layer(h, norm_mix[layer], mods, layer, b_w_q_b, b_w_o_b, j, kv, b_rel_bias[j])
        if layer == n_a - 1:
            h, kv = _ffn(h, norm_ffn[layer], mods, layer, ffn_w_in_b, ffn_w_out_b,
                         kv=(kv_norm, kv_mods, kv_w_b))
        elif layer == depth - 1:
            h = _ffn(h, norm_ffn[layer], mods, layer, ffn_w_in_b, ffn_w_out_b, final_gain=final_norm)
        else:
            h = _ffn(h, norm_ffn[layer], mods, layer, ffn_w_in_b, ffn_w_out_b)
    return h
```

```python
import functools

import numpy as np
import jax
import jax.numpy as jnp
from jax import lax
from jax.experimental import pallas as pl
from jax.experimental.pallas import tpu as pltpu

V7X_LANES = 128
V7X_SUBLANES = 8
V7X_MXU_DIM = 256
V7X_VMEM_LIMIT_BYTES = 56 * 1024 * 1024

CHUNK = 64
A_HEADS = 8
A_KEY_DIM = 128
B_HEADS = 16
B_HEAD_DIM = 64
B_PAST_CHUNKS = 8
REL_CLIP = 256
NORM_EPS = 1e-6
N_MOD = 6
MASK_VALUE = -1e30
MIN_FORGET = 1e-30
LOG2_E = 1.4426950408889634

SUB = V7X_SUBLANES
BLOCKS_PER_CHUNK = CHUNK // SUB
PAIR = 2 * A_KEY_DIM
N_OFF_PAIRS = BLOCKS_PER_CHUNK * (BLOCKS_PER_CHUNK - 1) // 2

HGRN_TILE = 256
ATT_TILE = 256
ATT_WIN = ATT_TILE + B_PAST_CHUNKS * CHUNK
ATT_GROUP = 4
ATT_GROUP_LANES = ATT_GROUP * B_HEAD_DIM
ROW_TILE = 512
MOD_MAX_TILE = 2048

f32 = jnp.float32
bf16 = jnp.bfloat16


def _params(*sem):
    return pltpu.CompilerParams(dimension_semantics=sem, vmem_limit_bytes=V7X_VMEM_LIMIT_BYTES)


def _resident(shape, index_map):
    return pl.BlockSpec(shape, index_map, pipeline_mode=pl.Buffered(1))


def _row_spec(d):
    return pl.BlockSpec((1, d), lambda b, i: (0, 0))


def _mod_spec(layer, col, d):
    return pl.BlockSpec((None, None, 1, d), lambda b, i: (layer, b, 0, col))


def _sigmoid(x):
    return 1.0 / (1.0 + jnp.exp(-x))


def _norm_mod(x, gain, shift, scale):
    y = x * lax.rsqrt(jnp.mean(x * x, axis=-1, keepdims=True) + NORM_EPS)
    return (y * gain) * (1.0 + scale) + shift


def _mod_kernel(c_ref, w_ref, b_ref, o_ref):
    c = c_ref[...]
    c_act = c * _sigmoid(c)
    o_ref[...] = jnp.dot(c_act, w_ref[...], preferred_element_type=f32,
                         precision=lax.Precision.HIGHEST) + b_ref[...]


def _modulation(c, w, b):
    n_layers, d, n = w.shape
    bsz = c.shape[0]
    tn = max(t for t in range(V7X_LANES, MOD_MAX_TILE + 1, V7X_LANES) if n % t == 0)
    return pl.pallas_call(
        _mod_kernel,
        grid=(n_layers, n // tn),
        in_specs=[
            pl.BlockSpec((bsz, d), lambda l, j: (0, 0)),
            pl.BlockSpec((None, d, tn), lambda l, j: (l, 0, j)),
            pl.BlockSpec((None, 1, tn), lambda l, j: (l, 0, j)),
        ],
        out_specs=pl.BlockSpec((None, bsz, tn), lambda l, j: (l, 0, j)),
        out_shape=jax.ShapeDtypeStruct((n_layers, bsz, n), f32),
        compiler_params=_params("arbitrary", "arbitrary"),
        name="modulation",
    )(c, w, b.reshape(n_layers, 1, n))


def _ffn_kernel(*refs, ffn_dim, final_norm, with_kv):
    h_ref, gain_ref, shift_ref, scale_ref, gate_ref, win_ref, wout_ref = refs[:7]
    rest = refs[7:]
    x = h_ref[...]
    u = _norm_mod(x, gain_ref[...], shift_ref[...], scale_ref[...]).astype(bf16)
    mid = (ffn_dim // 2) // V7X_MXU_DIM * V7X_MXU_DIM
    acc = None
    for lo, hi in ((0, mid), (mid, ffn_dim)):
        a = jnp.dot(u, win_ref[:, lo:hi], preferred_element_type=f32)
        b = jnp.dot(u, win_ref[:, ffn_dim + lo:ffn_dim + hi], preferred_element_type=f32)
        act = ((a * _sigmoid(a)) * b).astype(bf16)
        p = jnp.dot(act, wout_ref[lo:hi, :], preferred_element_type=f32)
        acc = p if acc is None else acc + p
    out = x + gate_ref[...] * acc
    if with_kv:
        kgain_ref, kshift_ref, kscale_ref, kvw_ref, o_ref, kv_ref = rest
        ukv = _norm_mod(out, kgain_ref[...], kshift_ref[...], kscale_ref[...]).astype(bf16)
        kv_ref[...] = jnp.dot(ukv, kvw_ref[...], preferred_element_type=f32).astype(kv_ref.dtype)
    elif final_norm:
        fgain_ref, o_ref = rest
        out = out * lax.rsqrt(jnp.mean(out * out, axis=-1, keepdims=True) + NORM_EPS)
        out = out * fgain_ref[...]
    else:
        (o_ref,) = rest
    o_ref[...] = out


def _ffn(h, gain, mods, layer, w_in, w_out, final_gain=None, kv=None):
    bsz, seq, d = h.shape
    ffn_dim = w_out.shape[1]
    tm = min(ROW_TILE, seq)
    h_spec = pl.BlockSpec((None, tm, d), lambda b, i: (b, i, 0))
    in_specs = [
        h_spec, _row_spec(d), _mod_spec(layer, 3, d), _mod_spec(layer, 4, d), _mod_spec(layer, 5, d),
        _resident((None, d, 2 * ffn_dim), lambda b, i: (layer, 0, 0)),
        _resident((None, ffn_dim, d), lambda b, i: (layer, 0, 0)),
    ]
    args = [h, gain.reshape(1, d), mods, mods, mods, w_in, w_out]
    out_specs = h_spec
    out_shape = jax.ShapeDtypeStruct((bsz, seq, d), f32)
    if kv is not None:
        kv_norm, kv_mods, kv_w = kv
        n_kv = kv_w.shape[-1]
        in_specs += [_row_spec(d), _mod_spec(0, 0, d), _mod_spec(0, 1, d),
                     _resident((d, n_kv), lambda b, i: (0, 0))]
        args += [kv_norm.reshape(1, d), kv_mods, kv_mods, kv_w]
        out_specs = (h_spec, pl.BlockSpec((None, tm, n_kv), lambda b, i: (b, i, 0)))
        out_shape = (out_shape, jax.ShapeDtypeStruct((bsz, seq, n_kv), bf16))
    elif final_gain is not None:
        in_specs += [_row_spec(d)]
        args += [final_gain.reshape(1, d)]
    return pl.pallas_call(
        functools.partial(_ffn_kernel, ffn_dim=ffn_dim, final_norm=final_gain is not None,
                          with_kv=kv is not None),
        grid=(bsz, seq // tm),
        in_specs=in_specs,
        out_specs=out_specs,
        out_shape=out_shape,
        compiler_params=_params("parallel", "arbitrary"),
        name="ffn",
    )(*args)


def _block_diag2(a, b):
    z = jnp.zeros_like(a)
    return jnp.concatenate([jnp.concatenate([a, z], axis=1),
                            jnp.concatenate([z, b], axis=1)], axis=0)


def _block_cumsum(x3):
    sub = lax.broadcasted_iota(jnp.int32, x3.shape, 1)
    for sh in (1, 2, 4):
        x3 = x3 + jnp.where(sub >= sh, pltpu.roll(x3, sh, axis=1), 0.0)
    return x3


def _hgrn_layer_kernel(h_ref, hnext_ref, gain_ref, shift_ref, scale_ref, gate_ref, win_ref, wout_ref,
                       alb_ref, ogain_ref, wsel_ref, wselw_ref, ones_ref, o_ref,
                       st_ref, y_scr, q_scr, w_scr, c_scr, qt_scr, kt_scr, vb_scr, sg_scr, ctmp_scr,
                       ctot_scr, ginc_scr, lhs_scr, diag_scr, o_scr, *, layer, tile):
    d = A_HEADS * A_KEY_DIM
    nb = tile // SUB
    n_chunks = tile // CHUNK
    n_pairs = A_HEADS // 2

    def in_proj(src_ref):
        u = _norm_mod(src_ref[...], gain_ref[...], shift_ref[...], scale_ref[...]).astype(bf16)
        y_scr[...] = jnp.dot(u, win_ref[...], preferred_element_type=f32)

    @pl.when(pl.program_id(1) == 0)
    def _():
        st_ref[...] = jnp.zeros_like(st_ref)
        in_proj(h_ref)

    a_lb = alb_ref[...]
    e_lb = jnp.exp(a_lb - jnp.max(a_lb, axis=0, keepdims=True))
    sm = e_lb / jnp.sum(e_lb, axis=0, keepdims=True)
    lb = jnp.zeros((1, d), f32)
    for m in range(1, layer + 1):
        lb = lb + sm[m:m + 1, :]
    oml = 1.0 - lb

    qpre = y_scr[:, 0:d]
    zf = y_scr[:, d:2 * d]
    q = qpre * _sigmoid(qpre)
    sig = _sigmoid(zf)
    log_f = jnp.log2(jnp.maximum(lb + oml * sig, MIN_FORGET))
    k_sign = jnp.where(oml < 0.0, -1.0, 1.0)
    log_k = jnp.log2(jnp.abs(oml) * (1.0 - sig))

    c3 = _block_cumsum(log_f.reshape(nb, SUB, d))
    ctot3 = c3[:, SUB - 1:SUB, :]
    w3 = c3 - log_k.reshape(nb, SUB, d)
    q_scr[...] = q * k_sign
    c_scr[...] = c3.reshape(tile, d)
    w_scr[...] = w3.reshape(tile, d)
    qt_scr[...] = (q.reshape(nb, SUB, d) * jnp.exp2(c3)).reshape(tile, d)
    kt_scr[...] = (jnp.exp2(ctot3 - w3).reshape(tile, d)) * k_sign
    vb_scr[...] = y_scr[:, 2 * d:3 * d].astype(bf16)
    g = y_scr[:, 3 * d:4 * d]
    sg_scr[...] = g * _sigmoid(g)
    in_proj(hnext_ref)

    for h in range(d // V7X_LANES):
        ctmp_scr[h] = c_scr[:, h * V7X_LANES:(h + 1) * V7X_LANES]
    ctot = jnp.concatenate([ctmp_scr[h, pl.ds(SUB - 1, nb, stride=SUB), :]
                            for h in range(d // V7X_LANES)], axis=1)
    ctot_scr[...] = ctot
    ginc_scr[...] = _block_cumsum(ctot.reshape(n_chunks, BLOCKS_PER_CHUNK, d)).reshape(nb, d)

    sub = lax.broadcasted_iota(jnp.int32, (nb, SUB, d), 1)
    for s in range(SUB):
        c3 = c_scr[...].reshape(nb, SUB, d)
        w_s = w_scr[...].reshape(nb, SUB, d)[:, s:s + 1, :]
        p = q_scr[...].reshape(nb, SUB, d) * jnp.exp2(jnp.where(sub >= s, c3 - w_s, MASK_VALUE))
        p = p.reshape(tile, d).astype(bf16)
        for m in range(n_pairs):
            lhs_scr[m * tile:(m + 1) * tile, s * PAIR:(s + 1) * PAIR] = p[:, m * PAIR:(m + 1) * PAIR]
    half = n_pairs * tile // 2
    diag_scr[0:half] = jnp.dot(lhs_scr[0:half], wsel_ref[...], preferred_element_type=f32)
    diag_scr[half:] = jnp.dot(lhs_scr[half:], wselw_ref[...],
                              preferred_element_type=f32)[:, :2 * CHUNK]

    lane = lax.broadcasted_iota(jnp.int32, (SUB, V7X_LANES), 1)
    row = lax.broadcasted_iota(jnp.int32, (SUB, V7X_LANES), 0)
    col_block = (lane % CHUNK) // SUB
    col_sub = lane % SUB

    def chunk_body(ci, carry):
        r0 = pl.multiple_of(ci * CHUNK, CHUNK)
        b0 = pl.multiple_of(ci * BLOCKS_PER_CHUNK, BLOCKS_PER_CHUNK)
        qt = qt_scr[pl.ds(r0, CHUNK), :]
        kt = kt_scr[pl.ds(r0, CHUNK), :]
        vb = vb_scr[pl.ds(r0, CHUNK), :]
        ginc = ginc_scr[pl.ds(b0, BLOCKS_PER_CHUNK), :]
        ctot_c = ctot_scr[pl.ds(b0, BLOCKS_PER_CHUNK), :]
        gexc = ginc - ctot_c
        btot = ginc[BLOCKS_PER_CHUNK - 1:BLOCKS_PER_CHUNK, :]
        e_g = jnp.exp2(gexc)
        e_r = jnp.exp2(btot - ginc)
        qh = jnp.concatenate([qt[SUB * i:SUB * (i + 1)] * e_g[i:i + 1] for i in range(BLOCKS_PER_CHUNK)],
                             axis=0).astype(bf16)
        kh = jnp.concatenate([kt[SUB * i:SUB * (i + 1)] * e_r[i:i + 1] for i in range(BLOCKS_PER_CHUNK)],
                             axis=0).astype(bf16)
        rows = []
        for i in range(1, BLOCKS_PER_CHUNK):
            d_i = jnp.exp2(gexc[i:i + 1] - ginc[0:i])
            for j in range(i):
                rows.append(qt[SUB * i:SUB * (i + 1)] * d_i[j:j + 1])
        lhs_off = jnp.concatenate(rows, axis=0).astype(bf16)
        e_b = jnp.exp2(btot)

        for m in range(n_pairs):
            lo, mid, hi = m * PAIR, m * PAIR + A_KEY_DIM, (m + 1) * PAIR
            k_bd_t = _block_diag2(kt[:, lo:mid], kt[:, mid:hi]).T.astype(bf16)
            off = jnp.dot(lhs_off[:, lo:hi], k_bd_t, preferred_element_type=f32)
            dg = diag_scr[pl.ds(pl.multiple_of(m * tile + r0, CHUNK), CHUNK), :]
            blocks = []
            for i in range(BLOCKS_PER_CHUNK):
                a = jnp.where(col_block == i, jnp.where(col_sub <= row, dg[SUB * i:SUB * (i + 1)], 0.0), 0.0)
                for j in range(i):
                    pidx = i * (i - 1) // 2 + j
                    a = jnp.where(col_block == j, off[SUB * pidx:SUB * (pidx + 1)], a)
                blocks.append(a)
            a2 = jnp.concatenate(blocks, axis=0).astype(bf16)
            v_bd = _block_diag2(vb[:, lo:mid], vb[:, mid:hi])
            o_intra = jnp.dot(a2, v_bd, preferred_element_type=f32)
            st_e = st_ref[2 * m]
            st_o = st_ref[2 * m + 1]
            st_bd = _block_diag2(st_e.T.astype(bf16), st_o.T.astype(bf16))
            o_inter = jnp.dot(qh[:, lo:hi], st_bd, preferred_element_type=f32)
            o_scr[pl.ds(r0, CHUNK), lo:hi] = o_intra + o_inter
            upd = lax.dot_general(vb[:, lo:hi], kh[:, lo:hi], (((0,), (0,)), ((), ())),
                                  preferred_element_type=f32)
            st_ref[2 * m] = e_b[:, lo:mid] * st_e + upd[:A_KEY_DIM, :A_KEY_DIM]
            st_ref[2 * m + 1] = e_b[:, mid:hi] * st_o + upd[A_KEY_DIM:, A_KEY_DIM:]
        return carry

    lax.fori_loop(0, n_chunks, chunk_body, 0, unroll=True)

    o = o_scr[...]
    osq = (o * o).astype(bf16)
    ms = jnp.concatenate(
        [jnp.dot(osq[:, m * PAIR:(m + 1) * PAIR], ones_ref[...], preferred_element_type=f32)
         for m in range(n_pairs)], axis=1) * (1.0 / A_KEY_DIM)
    out = o * lax.rsqrt(ms + NORM_EPS) * ogain_ref[...]
    out = (out * sg_scr[...]).astype(bf16)
    mix = jnp.dot(out, wout_ref[...], preferred_element_type=f32)
    o_ref[...] = h_ref[...] + gate_ref[...] * mix


def _hgrn_selector():
    r = np.arange(SUB * PAIR)
    s_local, h_r = r // PAIR, (r % PAIR) // A_KEY_DIM
    c = np.arange(2 * CHUNK)
    h_c, s_c = c // CHUNK, c % CHUNK
    sel = (h_r[:, None] == h_c[None, :]) & (s_local[:, None] == (s_c % SUB)[None, :])
    return jnp.asarray(sel, dtype=bf16)


def _hgrn_layer(h, gain, mods, layer, w_in, w_out, a_lb, o_gain):
    bsz, seq, d = h.shape
    assert d == A_HEADS * A_KEY_DIM
    tile = min(HGRN_TILE, seq)
    n_pairs = A_HEADS // 2
    n_in = w_in.shape[-1]
    ones_bd = jnp.asarray(np.kron(np.eye(2), np.ones((A_KEY_DIM, A_KEY_DIM))), dtype=bf16)
    gain_t = jnp.tile(o_gain.astype(f32), A_HEADS).reshape(1, d)
    n_a = a_lb.shape[0]
    sel = _hgrn_selector()
    h_spec = pl.BlockSpec((None, tile, d), lambda b, i: (b, i, 0))
    return pl.pallas_call(
        functools.partial(_hgrn_layer_kernel, layer=layer, tile=tile),
        grid=(bsz, seq // tile),
        in_specs=[
            h_spec,
            pl.BlockSpec((None, tile, d), lambda b, i: (b, jnp.minimum(i + 1, seq // tile - 1), 0)),
            _row_spec(d), _mod_spec(layer, 0, d), _mod_spec(layer, 1, d),
            _mod_spec(layer, 2, d),
            _resident((None, d, n_in), lambda b, i: (layer, 0, 0)),
            _resident((None, d, d), lambda b, i: (layer, 0, 0)),
            pl.BlockSpec((n_a, d), lambda b, i: (0, 0)),
            _row_spec(d),
            _resident((SUB * PAIR, 2 * CHUNK), lambda b, i: (0, 0)),
            _resident((SUB * PAIR, 4 * CHUNK), lambda b, i: (0, 0)),
            _resident((PAIR, PAIR), lambda b, i: (0, 0)),
        ],
        out_specs=h_spec,
        out_shape=jax.ShapeDtypeStruct((bsz, seq, d), f32),
        scratch_shapes=[
            pltpu.VMEM((A_HEADS, A_KEY_DIM, A_KEY_DIM), f32),
            pltpu.VMEM((tile, n_in), f32),
            pltpu.VMEM((tile, d), f32),
            pltpu.VMEM((tile, d), f32),
            pltpu.VMEM((tile, d), f32),
            pltpu.VMEM((tile, d), f32),
            pltpu.VMEM((tile, d), f32),
            pltpu.VMEM((tile, d), bf16),
            pltpu.VMEM((tile, d), f32),
            pltpu.VMEM((d // V7X_LANES, tile, V7X_LANES), f32),
            pltpu.VMEM((tile // SUB, d), f32),
            pltpu.VMEM((tile // SUB, d), f32),
            pltpu.VMEM((n_pairs * tile, SUB * PAIR), bf16),
            pltpu.VMEM((n_pairs * tile, 2 * CHUNK), f32),
            pltpu.VMEM((tile, d), f32),
        ],
        compiler_params=_params("parallel", "arbitrary"),
        name="hgrn_layer",
    )(h, h, gain.reshape(1, d), mods, mods, mods, w_in, w_out, a_lb.astype(f32), gain_t,
      sel, jnp.concatenate([sel, jnp.zeros_like(sel)], axis=1), ones_bd)


def _attn_layer_kernel(h_ref, hnext_ref, gain_ref, shift_ref, scale_ref, gate_ref, wq_ref, wo_ref,
                       k0_ref, k1_ref, k2_ref, v0_ref, v1_ref, v2_ref, bias_ref, o_ref,
                       q_scr, qnext_scr, p_scr, oa_scr):
    i = pl.program_id(1)

    def q_proj(src_ref):
        u = _norm_mod(src_ref[...], gain_ref[...], shift_ref[...], scale_ref[...]).astype(bf16)
        q = jnp.dot(u, wq_ref[...], preferred_element_type=f32) * (B_HEAD_DIM ** -0.5 * LOG2_E)
        qnext_scr[...] = q.astype(bf16)

    @pl.when(i == 0)
    def _():
        q_proj(h_ref)

    q_scr[...] = qnext_scr[...]
    q_proj(hnext_ref)

    lane = lax.broadcasted_iota(jnp.int32, (1, ATT_GROUP_LANES), 1)
    head_of_lane = lane // B_HEAD_DIM
    kcol = lax.broadcasted_iota(jnp.int32, (1, ATT_WIN), 1)
    pad_row = jnp.where(kcol >= (2 - i) * ATT_TILE, 0.0, MASK_VALUE)
    for g in range(B_HEADS // ATT_GROUP):
        lanes = slice(g * ATT_GROUP_LANES, (g + 1) * ATT_GROUP_LANES)
        q4 = q_scr[:, lanes]
        kwin = jnp.concatenate([k0_ref[:, lanes], k1_ref[:, lanes], k2_ref[:, lanes]], axis=0)
        vwin = jnp.concatenate([v0_ref[:, lanes], v1_ref[:, lanes], v2_ref[:, lanes]], axis=0)
        inv = None
        v_rows = []
        for hh in range(ATT_GROUP):
            mine = head_of_lane == hh
            qh = jnp.where(mine, q4, jnp.zeros_like(q4))
            s = lax.dot_general(qh, kwin, (((1,), (1,)), ((), ())), preferred_element_type=f32)
            s = s + bias_ref[g * ATT_GROUP + hh] + pad_row
            p = jnp.exp2(s - jnp.max(s, axis=-1, keepdims=True))
            r = 1.0 / jnp.sum(p, axis=-1, keepdims=True)
            inv = jnp.where(mine, r, 0.0) if inv is None else jnp.where(mine, r, inv)
            p_scr[:, hh * ATT_WIN:(hh + 1) * ATT_WIN] = p.astype(bf16)
            v_rows.append(jnp.where(mine, vwin, jnp.zeros_like(vwin)))
        o4 = jnp.dot(p_scr[...], jnp.concatenate(v_rows, axis=0), preferred_element_type=f32)
        oa_scr[:, lanes] = (o4 * inv).astype(bf16)
    mix = jnp.dot(oa_scr[...], wo_ref[...], preferred_element_type=f32)
    o_ref[...] = h_ref[...] + gate_ref[...] * mix


def _attn_bias_table(rel_bias):
    n_rel, n_heads = rel_bias.shape
    span = ATT_TILE + ATT_WIN
    j = np.arange(span)
    delta = np.where(j < ATT_WIN, j, j - span)
    idx = np.clip(delta - B_PAST_CHUNKS * CHUNK, -REL_CLIP, CHUNK - 1) + REL_CLIP
    e = (rel_bias.astype(f32) * LOG2_E)[jnp.asarray(idx)].T
    flat = jnp.tile(e, (1, ATT_TILE))[:, :ATT_TILE * (span - 1)]
    table = flat.reshape(n_heads, ATT_TILE, span - 1)[:, :, :ATT_WIN]
    qchunk = np.arange(ATT_TILE)[:, None] // CHUNK + B_PAST_CHUNKS
    back = qchunk - np.arange(ATT_WIN)[None, :] // CHUNK
    in_band = (back >= 0) & (back <= B_PAST_CHUNKS)
    return jnp.where(jnp.asarray(in_band)[None], table, MASK_VALUE)


def _attn_layer(h, gain, mods, layer, w_q, w_o, w_layer, kv, rel_bias):
    bsz, seq, d = h.shape
    assert seq % ATT_TILE == 0 and d == B_HEADS * B_HEAD_DIM
    bias = _attn_bias_table(rel_bias)
    h_spec = pl.BlockSpec((None, ATT_TILE, d), lambda b, i: (b, i, 0))

    def kv_spec(back, col):
        return pl.BlockSpec((None, ATT_TILE, d), lambda b, i: (b, jnp.maximum(i - back, 0), col))

    return pl.pallas_call(
        _attn_layer_kernel,
        grid=(bsz, seq // ATT_TILE),
        in_specs=[
            h_spec,
            pl.BlockSpec((None, ATT_TILE, d), lambda b, i: (b, jnp.minimum(i + 1, seq // ATT_TILE - 1), 0)),
            _row_spec(d), _mod_spec(layer, 0, d), _mod_spec(layer, 1, d),
            _mod_spec(layer, 2, d),
            _resident((None, d, d), lambda b, i: (w_layer, 0, 0)),
            _resident((None, d, d), lambda b, i: (w_layer, 0, 0)),
            kv_spec(2, 0), kv_spec(1, 0), kv_spec(0, 0),
            kv_spec(2, 1), kv_spec(1, 1), kv_spec(0, 1),
            _resident((B_HEADS, ATT_TILE, ATT_WIN), lambda b, i: (0, 0, 0)),
        ],
        out_specs=h_spec,
        out_shape=jax.ShapeDtypeStruct((bsz, seq, d), f32),
        scratch_shapes=[
            pltpu.VMEM((ATT_TILE, d), bf16),
            pltpu.VMEM((ATT_TILE, d), bf16),
            pltpu.VMEM((ATT_TILE, ATT_GROUP * ATT_WIN), bf16),
            pltpu.VMEM((ATT_TILE, d), bf16),
        ],
        compiler_params=_params("parallel", "arbitrary"),
        name="attn_layer",
    )(h, h, gain.reshape(1, d), mods, mods, mods, w_q, w_o, kv, kv, kv, kv, kv, kv, bias)


def kernel(x, c, mod_w, mod_b, norm_mix, norm_ffn, ffn_w_in, ffn_w_out, a_w_in, a_w_out, a_lb,
           a_out_norm, kv_norm, kv_mod_w, kv_mod_b, kv_w, b_w_q, b_w_o, b_rel_bias, final_norm):
    depth = mod_w.shape[0]
    n_a = a_w_in.shape[0]
    bsz = x.shape[0]

    mods = _modulation(c, mod_w, mod_b).reshape(depth, bsz, 1, -1)
    kv_mods = _modulation(c, kv_mod_w[None], kv_mod_b[None]).reshape(1, bsz, 1, -1)

    ffn_w_in_b = ffn_w_in.astype(bf16)
    ffn_w_out_b = ffn_w_out.astype(bf16)
    a_w_in_b = a_w_in.astype(bf16)
    a_w_out_b = a_w_out.astype(bf16)
    kv_w_b = kv_w.astype(bf16)
    b_w_q_b = b_w_q.astype(bf16)
    b_w_o_b = b_w_o.astype(bf16)

    h = x
    kv = None
    for layer in range(depth):
        if layer < n_a:
            h = _hgrn_layer(h, norm_mix[layer], mods, layer, a_w_in_b, a_w_out_b, a_lb,
                            a_out_norm[layer])
        else:
            j = layer - n_a
            h = _attn_layer(h, norm_mix[layer], mods, layer, b_w_q_b, b_w_o_b, j, kv, b_rel_bias[j])
        if layer == n_a - 1:
            h, kv = _ffn(h, norm_ffn[layer], mods, layer, ffn_w_in_b, ffn_w_out_b,
                         kv=(kv_norm, kv_mods, kv_w_b))
        elif layer == depth - 1:
            h = _ffn(h, norm_ffn[layer], mods, layer, ffn_w_in_b, ffn_w_out_b, final_gain=final_norm)
        else:
            h = _ffn(h, norm_ffn[layer], mods, layer, ffn_w_in_b, ffn_w_out_b)
    return h
```

```python
import functools

import numpy as np
import jax
import jax.numpy as jnp
from jax import lax
from jax.experimental import pallas as pl
from jax.experimental.pallas import tpu as pltpu

V7X_LANES = 128
V7X_SUBLANES = 8
V7X_MXU_DIM = 256
V7X_VMEM_LIMIT_BYTES = 56 * 1024 * 1024

CHUNK = 64
A_HEADS = 8
A_KEY_DIM = 128
B_HEADS = 16
B_HEAD_DIM = 64
B_PAST_CHUNKS = 8
REL_CLIP = 256
NORM_EPS = 1e-6
N_MOD = 6
MASK_VALUE = -1e30
MIN_FORGET = 1e-30
LOG2_E = 1.4426950408889634

SUB = V7X_SUBLANES
BLOCKS_PER_CHUNK = CHUNK // SUB
PAIR = 2 * A_KEY_DIM
N_OFF_PAIRS = BLOCKS_PER_CHUNK * (BLOCKS_PER_CHUNK - 1) // 2

HGRN_TILE = 256
ATT_TILE = 256
ATT_WIN = ATT_TILE + B_PAST_CHUNKS * CHUNK
ATT_GROUP = 4
ATT_GROUP_LANES = ATT_GROUP * B_HEAD_DIM
ATT_STAGE_HEADS = 8
ATT_ONES_ROWS = 16
ROW_TILE = 512
MOD_MAX_TILE = 2048

f32 = jnp.float32
bf16 = jnp.bfloat16


def _params(*sem):
    return pltpu.CompilerParams(dimension_semantics=sem, vmem_limit_bytes=V7X_VMEM_LIMIT_BYTES)


def _resident(shape, index_map):
    return pl.BlockSpec(shape, index_map, pipeline_mode=pl.Buffered(1))


def _row_spec(d):
    return pl.BlockSpec((1, d), lambda b, i: (0, 0))


def _mod_spec(layer, col, d):
    return pl.BlockSpec((None, None, 1, d), lambda b, i: (layer, b, 0, col))


def _sigmoid(x):
    return 1.0 / (1.0 + jnp.exp(-x))


def _norm_mod(x, gain, shift, scale):
    y = x * lax.rsqrt(jnp.mean(x * x, axis=-1, keepdims=True) + NORM_EPS)
    return (y * gain) * (1.0 + scale) + shift


def _mod_kernel(c_ref, w_ref, b_ref, o_ref):
    c = c_ref[...]
    c_act = c * _sigmoid(c)
    o_ref[...] = jnp.dot(c_act, w_ref[...], preferred_element_type=f32,
                         precision=lax.Precision.HIGHEST) + b_ref[...]


def _modulation(c, w, b):
    n_layers, d, n = w.shape
    bsz = c.shape[0]
    tn = max(t for t in range(V7X_LANES, MOD_MAX_TILE + 1, V7X_LANES) if n % t == 0)
    return pl.pallas_call(
        _mod_kernel,
        grid=(n_layers, n // tn),
        in_specs=[
            pl.BlockSpec((bsz, d), lambda l, j: (0, 0)),
            pl.BlockSpec((None, d, tn), lambda l, j: (l, 0, j)),
            pl.BlockSpec((None, 1, tn), lambda l, j: (l, 0, j)),
        ],
        out_specs=pl.BlockSpec((None, bsz, tn), lambda l, j: (l, 0, j)),
        out_shape=jax.ShapeDtypeStruct((n_layers, bsz, n), f32),
        compiler_params=_params("arbitrary", "arbitrary"),
        name="modulation",
    )(c, w, b.reshape(n_layers, 1, n))


def _ffn_kernel(*refs, ffn_dim, final_norm, with_kv):
    h_ref, gain_ref, shift_ref, scale_ref, gate_ref, win_ref, wout_ref = refs[:7]
    rest = refs[7:]
    x = h_ref[...]
    u = _norm_mod(x, gain_ref[...], shift_ref[...], scale_ref[...]).astype(bf16)
    mid = (ffn_dim // 2) // V7X_MXU_DIM * V7X_MXU_DIM
    acc = None
    for lo, hi in ((0, mid), (mid, ffn_dim)):
        a = jnp.dot(u, win_ref[:, lo:hi], preferred_element_type=f32)
        b = jnp.dot(u, win_ref[:, ffn_dim + lo:ffn_dim + hi], preferred_element_type=f32)
        act = ((a * _sigmoid(a)) * b).astype(bf16)
        p = jnp.dot(act, wout_ref[lo:hi, :], preferred_element_type=f32)
        acc = p if acc is None else acc + p
    out = x + gate_ref[...] * acc
    if with_kv:
        kgain_ref, kshift_ref, kscale_ref, kvw_ref, o_ref, k_ref, vt_ref = rest
        ukv = _norm_mod(out, kgain_ref[...], kshift_ref[...], kscale_ref[...]).astype(bf16)
        kv = jnp.dot(ukv, kvw_ref[...], preferred_element_type=f32)
        d_kv = k_ref.shape[-1]
        k_ref[...] = kv[:, :d_kv].astype(k_ref.dtype)
        vt_ref[...] = kv[:, d_kv:].T.astype(vt_ref.dtype)
    elif final_norm:
        fgain_ref, o_ref = rest
        out = out * lax.rsqrt(jnp.mean(out * out, axis=-1, keepdims=True) + NORM_EPS)
        out = out * fgain_ref[...]
    else:
        (o_ref,) = rest
    o_ref[...] = out


def _ffn(h, gain, mods, layer, w_in, w_out, final_gain=None, kv=None):
    bsz, seq, d = h.shape
    ffn_dim = w_out.shape[1]
    tm = min(ROW_TILE, seq)
    h_spec = pl.BlockSpec((None, tm, d), lambda b, i: (b, i, 0))
    in_specs = [
        h_spec, _row_spec(d), _mod_spec(layer, 3, d), _mod_spec(layer, 4, d), _mod_spec(layer, 5, d),
        _resident((None, d, 2 * ffn_dim), lambda b, i: (layer, 0, 0)),
        _resident((None, ffn_dim, d), lambda b, i: (layer, 0, 0)),
    ]
    args = [h, gain.reshape(1, d), mods, mods, mods, w_in, w_out]
    out_specs = h_spec
    out_shape = jax.ShapeDtypeStruct((bsz, seq, d), f32)
    if kv is not None:
        kv_norm, kv_mods, kv_w = kv
        n_kv = kv_w.shape[-1]
        in_specs += [_row_spec(d), _mod_spec(0, 0, d), _mod_spec(0, 1, d),
                     _resident((d, n_kv), lambda b, i: (0, 0))]
        args += [kv_norm.reshape(1, d), kv_mods, kv_mods, kv_w]
        out_specs = (h_spec, pl.BlockSpec((None, tm, n_kv // 2), lambda b, i: (b, i, 0)),
                     pl.BlockSpec((None, n_kv // 2, tm), lambda b, i: (b, 0, i)))
        out_shape = (out_shape, jax.ShapeDtypeStruct((bsz, seq, n_kv // 2), bf16),
                     jax.ShapeDtypeStruct((bsz, n_kv // 2, seq), bf16))
    elif final_gain is not None:
        in_specs += [_row_spec(d)]
        args += [final_gain.reshape(1, d)]
    return pl.pallas_call(
        functools.partial(_ffn_kernel, ffn_dim=ffn_dim, final_norm=final_gain is not None,
                          with_kv=kv is not None),
        grid=(bsz, seq // tm),
        in_specs=in_specs,
        out_specs=out_specs,
        out_shape=out_shape,
        compiler_params=_params("parallel", "arbitrary"),
        name="ffn",
    )(*args)


def _block_diag2(a, b):
    z = jnp.zeros_like(a)
    return jnp.concatenate([jnp.concatenate([a, z], axis=1),
                            jnp.concatenate([z, b], axis=1)], axis=0)


def _block_cumsum(x3):
    sub = lax.broadcasted_iota(jnp.int32, x3.shape, 1)
    for sh in (1, 2, 4):
        x3 = x3 + jnp.where(sub >= sh, pltpu.roll(x3, sh, axis=1), 0.0)
    return x3


def _hgrn_layer_kernel(h_ref, hnext_ref, gain_ref, shift_ref, scale_ref, gate_ref, win_ref, wout_ref,
                       alb_ref, ogain_ref, wsel_ref, wselw_ref, ones_ref, o_ref,
                       st_ref, y_scr, q_scr, w_scr, c_scr, qt_scr, kt_scr, vb_scr, sg_scr, ctmp_scr,
                       ctot_scr, ginc_scr, lhs_scr, diag_scr, o_scr, *, layer, tile):
    d = A_HEADS * A_KEY_DIM
    nb = tile // SUB
    n_chunks = tile // CHUNK
    n_pairs = A_HEADS // 2

    def in_proj(src_ref):
        u = _norm_mod(src_ref[...], gain_ref[...], shift_ref[...], scale_ref[...]).astype(bf16)
        y_scr[...] = jnp.dot(u, win_ref[...], preferred_element_type=f32)

    @pl.when(pl.program_id(1) == 0)
    def _():
        st_ref[...] = jnp.zeros_like(st_ref)
        in_proj(h_ref)

    a_lb = alb_ref[...]
    e_lb = jnp.exp(a_lb - jnp.max(a_lb, axis=0, keepdims=True))
    sm = e_lb / jnp.sum(e_lb, axis=0, keepdims=True)
    lb = jnp.zeros((1, d), f32)
    for m in range(1, layer + 1):
        lb = lb + sm[m:m + 1, :]
    oml = 1.0 - lb

    qpre = y_scr[:, 0:d]
    zf = y_scr[:, d:2 * d]
    q = qpre * _sigmoid(qpre)
    sig = _sigmoid(zf)
    log_f = jnp.log2(jnp.maximum(lb + oml * sig, MIN_FORGET))
    k_sign = jnp.where(oml < 0.0, -1.0, 1.0)
    log_k = jnp.log2(jnp.abs(oml) * (1.0 - sig))

    c3 = _block_cumsum(log_f.reshape(nb, SUB, d))
    ctot3 = c3[:, SUB - 1:SUB, :]
    w3 = c3 - log_k.reshape(nb, SUB, d)
    q_scr[...] = q * k_sign
    c_scr[...] = c3.reshape(tile, d)
    w_scr[...] = w3.reshape(tile, d)
    qt_scr[...] = (q.reshape(nb, SUB, d) * jnp.exp2(c3)).reshape(tile, d)
    kt_scr[...] = (jnp.exp2(ctot3 - w3).reshape(tile, d)) * k_sign
    vb_scr[...] = y_scr[:, 2 * d:3 * d].astype(bf16)
    g = y_scr[:, 3 * d:4 * d]
    sg_scr[...] = g * _sigmoid(g)
    in_proj(hnext_ref)

    for h in range(d // V7X_LANES):
        ctmp_scr[h] = c_scr[:, h * V7X_LANES:(h + 1) * V7X_LANES]
    ctot = jnp.concatenate([ctmp_scr[h, pl.ds(SUB - 1, nb, stride=SUB), :]
                            for h in range(d // V7X_LANES)], axis=1)
    ctot_scr[...] = ctot
    ginc_scr[...] = _block_cumsum(ctot.reshape(n_chunks, BLOCKS_PER_CHUNK, d)).reshape(nb, d)

    sub = lax.broadcasted_iota(jnp.int32, (nb, SUB, d), 1)
    for s in range(SUB):
        c3 = c_scr[...].reshape(nb, SUB, d)
        w_s = w_scr[...].reshape(nb, SUB, d)[:, s:s + 1, :]
        p = q_scr[...].reshape(nb, SUB, d) * jnp.exp2(jnp.where(sub >= s, c3 - w_s, MASK_VALUE))
        p = p.reshape(tile, d).astype(bf16)
        for m in range(n_pairs):
            lhs_scr[m * tile:(m + 1) * tile, s * PAIR:(s + 1) * PAIR] = p[:, m * PAIR:(m + 1) * PAIR]
    half = n_pairs * tile // 2
    diag_scr[0:half] = jnp.dot(lhs_scr[0:half], wsel_ref[...], preferred_element_type=f32)
    diag_scr[half:] = jnp.dot(lhs_scr[half:], wselw_ref[...],
                              preferred_element_type=f32)[:, :2 * CHUNK]

    lane = lax.broadcasted_iota(jnp.int32, (SUB, V7X_LANES), 1)
    row = lax.broadcasted_iota(jnp.int32, (SUB, V7X_LANES), 0)
    col_block = (lane % CHUNK) // SUB
    col_sub = lane % SUB

    def chunk_body(ci, carry):
        r0 = pl.multiple_of(ci * CHUNK, CHUNK)
        b0 = pl.multiple_of(ci * BLOCKS_PER_CHUNK, BLOCKS_PER_CHUNK)
        qt = qt_scr[pl.ds(r0, CHUNK), :]
        kt = kt_scr[pl.ds(r0, CHUNK), :]
        vb = vb_scr[pl.ds(r0, CHUNK), :]
        ginc = ginc_scr[pl.ds(b0, BLOCKS_PER_CHUNK), :]
        ctot_c = ctot_scr[pl.ds(b0, BLOCKS_PER_CHUNK), :]
        gexc = ginc - ctot_c
        btot = ginc[BLOCKS_PER_CHUNK - 1:BLOCKS_PER_CHUNK, :]
        e_g = jnp.exp2(gexc)
        e_r = jnp.exp2(btot - ginc)
        qh = jnp.concatenate([qt[SUB * i:SUB * (i + 1)] * e_g[i:i + 1] for i in range(BLOCKS_PER_CHUNK)],
                             axis=0).astype(bf16)
        kh = jnp.concatenate([kt[SUB * i:SUB * (i + 1)] * e_r[i:i + 1] for i in range(BLOCKS_PER_CHUNK)],
                             axis=0).astype(bf16)
        rows = []
        for i in range(1, BLOCKS_PER_CHUNK):
            d_i = jnp.exp2(gexc[i:i + 1] - ginc[0:i])
            for j in range(i):
                rows.append(qt[SUB * i:SUB * (i + 1)] * d_i[j:j + 1])
        lhs_off = jnp.concatenate(rows, axis=0).astype(bf16)
        ktb = kt.astype(bf16)
        e_b = jnp.exp2(btot)

        for m in range(n_pairs):
            lo, mid, hi = m * PAIR, m * PAIR + A_KEY_DIM, (m + 1) * PAIR
            k_bd = _block_diag2(ktb[:, lo:mid], ktb[:, mid:hi])
            off = lax.dot_general(lhs_off[:, lo:hi], k_bd, (((1,), (1,)), ((), ())),
                                  preferred_element_type=f32)
            dg = diag_scr[pl.ds(pl.multiple_of(m * tile + r0, CHUNK), CHUNK), :]
            blocks = []
            for i in range(BLOCKS_PER_CHUNK):
                a = jnp.where(col_block == i, jnp.where(col_sub <= row, dg[SUB * i:SUB * (i + 1)], 0.0), 0.0)
                for j in range(i):
                    pidx = i * (i - 1) // 2 + j
                    a = jnp.where(col_block == j, off[SUB * pidx:SUB * (pidx + 1)], a)
                blocks.append(a)
            a2 = jnp.concatenate(blocks, axis=0).astype(bf16)
            v_bd = _block_diag2(vb[:, lo:mid], vb[:, mid:hi])
            o_intra = jnp.dot(a2, v_bd, preferred_element_type=f32)
            st_e = st_ref[2 * m]
            st_o = st_ref[2 * m + 1]
            st_bd = _block_diag2(st_e.astype(bf16), st_o.astype(bf16))
            o_inter = lax.dot_general(qh[:, lo:hi], st_bd, (((1,), (1,)), ((), ())),
                                      preferred_element_type=f32)
            o_scr[pl.ds(r0, CHUNK), lo:hi] = o_intra + o_inter
            upd = lax.dot_general(vb[:, lo:hi], kh[:, lo:hi], (((0,), (0,)), ((), ())),
                                  preferred_element_type=f32)
            st_ref[2 * m] = e_b[:, lo:mid] * st_e + upd[:A_KEY_DIM, :A_KEY_DIM]
            st_ref[2 * m + 1] = e_b[:, mid:hi] * st_o + upd[A_KEY_DIM:, A_KEY_DIM:]
        return carry

    lax.fori_loop(0, n_chunks, chunk_body, 0, unroll=True)

    o = o_scr[...]
    osq = (o * o).astype(bf16)
    ms = jnp.concatenate(
        [jnp.dot(osq[:, m * PAIR:(m + 1) * PAIR], ones_ref[...], preferred_element_type=f32)
         for m in range(n_pairs)], axis=1) * (1.0 / A_KEY_DIM)
    out = o * lax.rsqrt(ms + NORM_EPS) * ogain_ref[...]
    out = (out * sg_scr[...]).astype(bf16)
    mix = jnp.dot(out, wout_ref[...], preferred_element_type=f32)
    o_ref[...] = h_ref[...] + gate_ref[...] * mix


def _hgrn_selector():
    r = np.arange(SUB * PAIR)
    s_local, h_r = r // PAIR, (r % PAIR) // A_KEY_DIM
    c = np.arange(2 * CHUNK)
    h_c, s_c = c // CHUNK, c % CHUNK
    sel = (h_r[:, None] == h_c[None, :]) & (s_local[:, None] == (s_c % SUB)[None, :])
    return jnp.asarray(sel, dtype=bf16)


def _hgrn_layer(h, gain, mods, layer, w_in, w_out, a_lb, o_gain):
    bsz, seq, d = h.shape
    assert d == A_HEADS * A_KEY_DIM
    tile = min(HGRN_TILE, seq)
    n_pairs = A_HEADS // 2
    n_in = w_in.shape[-1]
    ones_bd = jnp.asarray(np.kron(np.eye(2), np.ones((A_KEY_DIM, A_KEY_DIM))), dtype=bf16)
    gain_t = jnp.tile(o_gain.astype(f32), A_HEADS).reshape(1, d)
    n_a = a_lb.shape[0]
    sel = _hgrn_selector()
    h_spec = pl.BlockSpec((None, tile, d), lambda b, i: (b, i, 0))
    return pl.pallas_call(
        functools.partial(_hgrn_layer_kernel, layer=layer, tile=tile),
        grid=(bsz, seq // tile),
        in_specs=[
            h_spec,
            pl.BlockSpec((None, tile, d), lambda b, i: (b, jnp.minimum(i + 1, seq // tile - 1), 0)),
            _row_spec(d), _mod_spec(layer, 0, d), _mod_spec(layer, 1, d),
            _mod_spec(layer, 2, d),
            _resident((None, d, n_in), lambda b, i: (layer, 0, 0)),
            _resident((None, d, d), lambda b, i: (layer, 0, 0)),
            pl.BlockSpec((n_a, d), lambda b, i: (0, 0)),
            _row_spec(d),
            _resident((SUB * PAIR, 2 * CHUNK), lambda b, i: (0, 0)),
            _resident((SUB * PAIR, 4 * CHUNK), lambda b, i: (0, 0)),
            _resident((PAIR, PAIR), lambda b, i: (0, 0)),
        ],
        out_specs=h_spec,
        out_shape=jax.ShapeDtypeStruct((bsz, seq, d), f32),
        scratch_shapes=[
            pltpu.VMEM((A_HEADS, A_KEY_DIM, A_KEY_DIM), f32),
            pltpu.VMEM((tile, n_in), f32),
            pltpu.VMEM((tile, d), f32),
            pltpu.VMEM((tile, d), f32),
            pltpu.VMEM((tile, d), f32),
            pltpu.VMEM((tile, d), f32),
            pltpu.VMEM((tile, d), f32),
            pltpu.VMEM((tile, d), bf16),
            pltpu.VMEM((tile, d), f32),
            pltpu.VMEM((d // V7X_LANES, tile, V7X_LANES), f32),
            pltpu.VMEM((tile // SUB, d), f32),
            pltpu.VMEM((tile // SUB, d), f32),
            pltpu.VMEM((n_pairs * tile, SUB * PAIR), bf16),
            pltpu.VMEM((n_pairs * tile, 2 * CHUNK), f32),
            pltpu.VMEM((tile, d), f32),
        ],
        compiler_params=_params("parallel", "arbitrary"),
        name="hgrn_layer",
    )(h, h, gain.reshape(1, d), mods, mods, mods, w_in, w_out, a_lb.astype(f32), gain_t,
      sel, jnp.concatenate([sel, jnp.zeros_like(sel)], axis=1), ones_bd)


def _attn_layer_kernel(h_ref, hnext_ref, gain_ref, shift_ref, scale_ref, gate_ref, wq_ref, wo_ref,
                       k0_ref, k1_ref, k2_ref, vt0_ref, vt1_ref, vt2_ref, bias_ref, o_ref,
                       q_scr, qnext_scr, biasp_scr, ot_scr):
    i = pl.program_id(1)

    def q_proj(src_ref):
        u = _norm_mod(src_ref[...], gain_ref[...], shift_ref[...], scale_ref[...]).astype(bf16)
        q = jnp.dot(u, wq_ref[...], preferred_element_type=f32) * (B_HEAD_DIM ** -0.5 * LOG2_E)
        qnext_scr[...] = q.astype(bf16)

    @pl.when(i == 0)
    def _():
        q_proj(h_ref)

    @pl.when(i <= ATT_WIN // ATT_TILE - 1)
    def _():
        krow = lax.broadcasted_iota(jnp.int32, (ATT_WIN, ATT_TILE), 0)
        pad = jnp.where(krow >= (ATT_WIN // ATT_TILE - 1 - i) * ATT_TILE, 0.0, MASK_VALUE)
        for hd in range(B_HEADS):
            biasp_scr[hd] = bias_ref[hd] + pad

    q_scr[...] = qnext_scr[...]
    q_proj(hnext_ref)

    lane = lax.broadcasted_iota(jnp.int32, (1, ATT_GROUP_LANES), 1)
    head_of_lane = lane // B_HEAD_DIM
    ones_rows = jnp.ones((ATT_ONES_ROWS, ATT_WIN), bf16)
    for h0 in range(0, B_HEADS, ATT_STAGE_HEADS):
        heads = list(range(h0, h0 + ATT_STAGE_HEADS))
        s_ts = []
        for head in heads:
            g, hh = divmod(head, ATT_GROUP)
            lanes = slice(g * ATT_GROUP_LANES, (g + 1) * ATT_GROUP_LANES)
            kwin = jnp.concatenate([k0_ref[:, lanes], k1_ref[:, lanes], k2_ref[:, lanes]], axis=0)
            keep = jnp.where(head_of_lane == hh, 1.0, 0.0).astype(bf16)
            s_ts.append(lax.dot_general(kwin, q_scr[:, lanes] * keep, (((1,), (1,)), ((), ())),
                                        preferred_element_type=f32))
        p_ts = []
        for head, s_t in zip(heads, s_ts):
            s_t = s_t + biasp_scr[head]
            p_ts.append(jnp.exp2(s_t - jnp.max(s_t, axis=0, keepdims=True)).astype(bf16))
        for head, p_t in zip(heads, p_ts):
            rows = slice(head * B_HEAD_DIM, (head + 1) * B_HEAD_DIM)
            v_t = jnp.concatenate([vt0_ref[rows, :], vt1_ref[rows, :], vt2_ref[rows, :]], axis=1)
            o_t = jnp.dot(jnp.concatenate([v_t, ones_rows], axis=0), p_t,
                          preferred_element_type=f32)
            ot_scr[rows, :] = o_t[:B_HEAD_DIM] * (1.0 / o_t[B_HEAD_DIM:B_HEAD_DIM + 1])
    mix = lax.dot_general(ot_scr[...].astype(bf16), wo_ref[...], (((0,), (0,)), ((), ())),
                          preferred_element_type=f32)
    o_ref[...] = h_ref[...] + gate_ref[...] * mix


def _attn_bias_table(rel_bias):
    n_rel, n_heads = rel_bias.shape
    span = ATT_TILE + ATT_WIN
    j = np.arange(span)
    delta = np.where(j < ATT_WIN, j, j - span)
    idx = np.clip(delta - B_PAST_CHUNKS * CHUNK, -REL_CLIP, CHUNK - 1) + REL_CLIP
    e = (rel_bias.astype(f32) * LOG2_E)[jnp.asarray(idx)].T
    flat = jnp.tile(e, (1, ATT_TILE))[:, :ATT_TILE * (span - 1)]
    table = flat.reshape(n_heads, ATT_TILE, span - 1)[:, :, :ATT_WIN]
    qchunk = np.arange(ATT_TILE)[:, None] // CHUNK + B_PAST_CHUNKS
    back = qchunk - np.arange(ATT_WIN)[None, :] // CHUNK
    in_band = (back >= 0) & (back <= B_PAST_CHUNKS)
    return jnp.swapaxes(jnp.where(jnp.asarray(in_band)[None], table, MASK_VALUE), 1, 2)


def _attn_layer(h, gain, mods, layer, w_q, w_o, w_layer, k, v_t, rel_bias):
    bsz, seq, d = h.shape
    assert seq % ATT_TILE == 0 and d == B_HEADS * B_HEAD_DIM
    bias = _attn_bias_table(rel_bias)
    h_spec = pl.BlockSpec((None, ATT_TILE, d), lambda b, i: (b, i, 0))

    def k_spec(back):
        return pl.BlockSpec((None, ATT_TILE, d), lambda b, i: (b, jnp.maximum(i - back, 0), 0))

    def vt_spec(back):
        return pl.BlockSpec((None, d, ATT_TILE), lambda b, i: (b, 0, jnp.maximum(i - back, 0)))

    return pl.pallas_call(
        _attn_layer_kernel,
        grid=(bsz, seq // ATT_TILE),
        in_specs=[
            h_spec,
            pl.BlockSpec((None, ATT_TILE, d), lambda b, i: (b, jnp.minimum(i + 1, seq // ATT_TILE - 1), 0)),
            _row_spec(d), _mod_spec(layer, 0, d), _mod_spec(layer, 1, d),
            _mod_spec(layer, 2, d),
            _resident((None, d, d), lambda b, i: (w_layer, 0, 0)),
            _resident((None, d, d), lambda b, i: (w_layer, 0, 0)),
            k_spec(2), k_spec(1), k_spec(0),
            vt_spec(2), vt_spec(1), vt_spec(0),
            _resident((B_HEADS, ATT_WIN, ATT_TILE), lambda b, i: (0, 0, 0)),
        ],
        out_specs=h_spec,
        out_shape=jax.ShapeDtypeStruct((bsz, seq, d), f32),
        scratch_shapes=[
            pltpu.VMEM((ATT_TILE, d), bf16),
            pltpu.VMEM((ATT_TILE, d), bf16),
            pltpu.VMEM((B_HEADS, ATT_WIN, ATT_TILE), f32),
            pltpu.VMEM((d, ATT_TILE), f32),
        ],
        compiler_params=_params("parallel", "arbitrary"),
        name="attn_layer",
    )(h, h, gain.reshape(1, d), mods, mods, mods, w_q, w_o, k, k, k, v_t, v_t, v_t, bias)


def kernel(x, c, mod_w, mod_b, norm_mix, norm_ffn, ffn_w_in, ffn_w_out, a_w_in, a_w_out, a_lb,
           a_out_norm, kv_norm, kv_mod_w, kv_mod_b, kv_w, b_w_q, b_w_o, b_rel_bias, final_norm):
    depth = mod_w.shape[0]
    n_a = a_w_in.shape[0]
    bsz = x.shape[0]

    mods = _modulation(c, mod_w, mod_b).reshape(depth, bsz, 1, -1)
    kv_mods = _modulation(c, kv_mod_w[None], kv_mod_b[None]).reshape(1, bsz, 1, -1)

    ffn_w_in_b = ffn_w_in.astype(bf16)
    ffn_w_out_b = ffn_w_out.astype(bf16)
    a_w_in_b = a_w_in.astype(bf16)
    a_w_out_b = a_w_out.astype(bf16)
    kv_w_b = kv_w.astype(bf16)
    b_w_q_b = b_w_q.astype(bf16)
    b_w_o_b = b_w_o.astype(bf16)

    h = x
    k = v_t = None
    for layer in range(depth):
        if layer < n_a:
            h = _hgrn_layer(h, norm_mix[layer], mods, layer, a_w_in_b, a_w_out_b, a_lb,
                            a_out_norm[layer])
        else:
            j = layer - n_a
            h = _attn_layer(h, norm_mix[layer], mods, layer, b_w_q_b, b_w_o_b, j, k, v_t,
                            b_rel_bias[j])
        if layer == n_a - 1:
            h, k, v_t = _ffn(h, norm_ffn[layer], mods, layer, ffn_w_in_b, ffn_w_out_b,
                         kv=(kv_norm, kv_mods, kv_w_b))
        elif layer == depth - 1:
            h = _ffn(h, norm_ffn[layer], mods, layer, ffn_w_in_b, ffn_w_out_b, final_gain=final_norm)
        else:
            h = _ffn(h, norm_ffn[layer], mods, layer, ffn_w_in_b, ffn_w_out_b)
    return h
```

```python
import functools

import numpy as np
import jax
import jax.numpy as jnp
from jax import lax
from jax.experimental import pallas as pl
from jax.experimental.pallas import tpu as pltpu

V7X_LANES = 128
V7X_SUBLANES = 8
V7X_MXU_DIM = 256
V7X_VMEM_LIMIT_BYTES = 56 * 1024 * 1024

CHUNK = 64
A_HEADS = 8
A_KEY_DIM = 128
B_HEADS = 16
B_HEAD_DIM = 64
B_PAST_CHUNKS = 8
REL_CLIP = 256
NORM_EPS = 1e-6
N_MOD = 6
MASK_VALUE = -1e30
MIN_FORGET = 1e-30
LOG2_E = 1.4426950408889634

SUB = V7X_SUBLANES
BLOCKS_PER_CHUNK = CHUNK // SUB
PAIR = 2 * A_KEY_DIM
N_OFF_PAIRS = BLOCKS_PER_CHUNK * (BLOCKS_PER_CHUNK - 1) // 2

HGRN_TILE = 256
ATT_TILE = 256
ATT_WIN = ATT_TILE + B_PAST_CHUNKS * CHUNK
ATT_GROUP = 4
ATT_GROUP_LANES = ATT_GROUP * B_HEAD_DIM
ATT_STAGE_HEADS = 8
ATT_ONES_ROWS = 16
ROW_TILE = 512
MOD_MAX_TILE = 2048

f32 = jnp.float32
bf16 = jnp.bfloat16


def _params(*sem):
    return pltpu.CompilerParams(dimension_semantics=sem, vmem_limit_bytes=V7X_VMEM_LIMIT_BYTES)


def _resident(shape, index_map):
    return pl.BlockSpec(shape, index_map, pipeline_mode=pl.Buffered(1))


def _row_spec(d):
    return pl.BlockSpec((1, d), lambda b, i: (0, 0))


def _mod_spec(layer, col, d):
    return pl.BlockSpec((None, None, 1, d), lambda b, i: (layer, b, 0, col))


def _sigmoid(x):
    return 1.0 / (1.0 + jnp.exp(-x))


def _norm_mod(x, gain, shift, scale):
    y = x * lax.rsqrt(jnp.mean(x * x, axis=-1, keepdims=True) + NORM_EPS)
    return (y * gain) * (1.0 + scale) + shift


def _mod_kernel(c_ref, w_ref, b_ref, o_ref):
    c = c_ref[...]
    c_act = c * _sigmoid(c)
    o_ref[...] = jnp.dot(c_act, w_ref[...], preferred_element_type=f32,
                         precision=lax.Precision.HIGHEST) + b_ref[...]


def _modulation(c, w, b):
    n_layers, d, n = w.shape
    bsz = c.shape[0]
    tn = max(t for t in range(V7X_LANES, MOD_MAX_TILE + 1, V7X_LANES) if n % t == 0)
    return pl.pallas_call(
        _mod_kernel,
        grid=(n_layers, n // tn),
        in_specs=[
            pl.BlockSpec((bsz, d), lambda l, j: (0, 0)),
            pl.BlockSpec((None, d, tn), lambda l, j: (l, 0, j)),
            pl.BlockSpec((None, 1, tn), lambda l, j: (l, 0, j)),
        ],
        out_specs=pl.BlockSpec((None, bsz, tn), lambda l, j: (l, 0, j)),
        out_shape=jax.ShapeDtypeStruct((n_layers, bsz, n), f32),
        compiler_params=_params("arbitrary", "arbitrary"),
        name="modulation",
    )(c, w, b.reshape(n_layers, 1, n))


def _ffn_kernel(*refs, ffn_dim, final_norm, with_kv):
    h_ref, gain_ref, shift_ref, scale_ref, gate_ref, win_ref, wout_ref = refs[:7]
    rest = refs[7:]
    x = h_ref[...]
    u = _norm_mod(x, gain_ref[...], shift_ref[...], scale_ref[...]).astype(bf16)
    mid = (ffn_dim // 2) // V7X_MXU_DIM * V7X_MXU_DIM
    acc = None
    for lo, hi in ((0, mid), (mid, ffn_dim)):
        a = jnp.dot(u, win_ref[:, lo:hi], preferred_element_type=f32)
        b = jnp.dot(u, win_ref[:, ffn_dim + lo:ffn_dim + hi], preferred_element_type=f32)
        act = ((a * _sigmoid(a)) * b).astype(bf16)
        p = jnp.dot(act, wout_ref[lo:hi, :], preferred_element_type=f32)
        acc = p if acc is None else acc + p
    out = x + gate_ref[...] * acc
    if with_kv:
        kgain_ref, kshift_ref, kscale_ref, kvw_ref, o_ref, k_ref, vt_ref = rest
        ukv = _norm_mod(out, kgain_ref[...], kshift_ref[...], kscale_ref[...]).astype(bf16)
        kv = jnp.dot(ukv, kvw_ref[...], preferred_element_type=f32)
        d_kv = k_ref.shape[-1]
        k_ref[...] = kv[:, :d_kv].astype(k_ref.dtype)
        vt_ref[...] = kv[:, d_kv:].T.astype(vt_ref.dtype)
    elif final_norm:
        fgain_ref, o_ref = rest
        out = out * lax.rsqrt(jnp.mean(out * out, axis=-1, keepdims=True) + NORM_EPS)
        out = out * fgain_ref[...]
    else:
        (o_ref,) = rest
    o_ref[...] = out


def _ffn(h, gain, mods, layer, w_in, w_out, final_gain=None, kv=None):
    bsz, seq, d = h.shape
    ffn_dim = w_out.shape[1]
    tm = min(ROW_TILE, seq)
    h_spec = pl.BlockSpec((None, tm, d), lambda b, i: (b, i, 0))
    in_specs = [
        h_spec, _row_spec(d), _mod_spec(layer, 3, d), _mod_spec(layer, 4, d), _mod_spec(layer, 5, d),
        _resident((None, d, 2 * ffn_dim), lambda b, i: (layer, 0, 0)),
        _resident((None, ffn_dim, d), lambda b, i: (layer, 0, 0)),
    ]
    args = [h, gain.reshape(1, d), mods, mods, mods, w_in, w_out]
    out_specs = h_spec
    out_shape = jax.ShapeDtypeStruct((bsz, seq, d), f32)
    if kv is not None:
        kv_norm, kv_mods, kv_w = kv
        n_kv = kv_w.shape[-1]
        in_specs += [_row_spec(d), _mod_spec(0, 0, d), _mod_spec(0, 1, d),
                     _resident((d, n_kv), lambda b, i: (0, 0))]
        args += [kv_norm.reshape(1, d), kv_mods, kv_mods, kv_w]
        out_specs = (h_spec, pl.BlockSpec((None, tm, n_kv // 2), lambda b, i: (b, i, 0)),
                     pl.BlockSpec((None, n_kv // 2, tm), lambda b, i: (b, 0, i)))
        out_shape = (out_shape, jax.ShapeDtypeStruct((bsz, seq, n_kv // 2), bf16),
                     jax.ShapeDtypeStruct((bsz, n_kv // 2, seq), bf16))
    elif final_gain is not None:
        in_specs += [_row_spec(d)]
        args += [final_gain.reshape(1, d)]
    return pl.pallas_call(
        functools.partial(_ffn_kernel, ffn_dim=ffn_dim, final_norm=final_gain is not None,
                          with_kv=kv is not None),
        grid=(bsz, seq // tm),
        in_specs=in_specs,
        out_specs=out_specs,
        out_shape=out_shape,
        compiler_params=_params("parallel", "arbitrary"),
        name="ffn",
    )(*args)


def _block_diag2(a, b):
    z = jnp.zeros_like(a)
    return jnp.concatenate([jnp.concatenate([a, z], axis=1),
                            jnp.concatenate([z, b], axis=1)], axis=0)


def _block_cumsum(x3):
    sub = lax.broadcasted_iota(jnp.int32, x3.shape, 1)
    for sh in (1, 2, 4):
        x3 = x3 + jnp.where(sub >= sh, pltpu.roll(x3, sh, axis=1), 0.0)
    return x3


def _hgrn_layer_kernel(h_ref, hnext_ref, gain_ref, shift_ref, scale_ref, gate_ref, win_ref, wout_ref,
                       alb_ref, ogain_ref, wsel_ref, wselw_ref, ones_ref, o_ref,
                       st_ref, y_scr, q_scr, w_scr, c_scr, qt_scr, kt_scr, vb_scr, sg_scr, ctmp_scr,
                       ctot_scr, ginc_scr, lhs_scr, diag_scr, o_scr, *, layer, tile):
    d = A_HEADS * A_KEY_DIM
    nb = tile // SUB
    n_chunks = tile // CHUNK
    n_pairs = A_HEADS // 2

    def in_proj(src_ref):
        u = _norm_mod(src_ref[...], gain_ref[...], shift_ref[...], scale_ref[...]).astype(bf16)
        y_scr[...] = jnp.dot(u, win_ref[...], preferred_element_type=f32)

    @pl.when(pl.program_id(1) == 0)
    def _():
        st_ref[...] = jnp.zeros_like(st_ref)
        in_proj(h_ref)

    a_lb = alb_ref[...]
    e_lb = jnp.exp(a_lb - jnp.max(a_lb, axis=0, keepdims=True))
    sm = e_lb / jnp.sum(e_lb, axis=0, keepdims=True)
    lb = jnp.zeros((1, d), f32)
    for m in range(1, layer + 1):
        lb = lb + sm[m:m + 1, :]
    oml = 1.0 - lb

    qpre = y_scr[:, 0:d]
    zf = y_scr[:, d:2 * d]
    q = qpre * _sigmoid(qpre)
    sig = _sigmoid(zf)
    log_f = jnp.log2(jnp.maximum(lb + oml * sig, MIN_FORGET))
    k_sign = jnp.where(oml < 0.0, -1.0, 1.0)
    log_k = jnp.log2(jnp.abs(oml) * (1.0 - sig))

    c3 = _block_cumsum(log_f.reshape(nb, SUB, d))
    ctot3 = c3[:, SUB - 1:SUB, :]
    w3 = c3 - log_k.reshape(nb, SUB, d)
    q_scr[...] = q * k_sign
    c_scr[...] = c3.reshape(tile, d)
    w_scr[...] = w3.reshape(tile, d)
    qt_scr[...] = (q.reshape(nb, SUB, d) * jnp.exp2(c3)).reshape(tile, d)
    kt_scr[...] = (jnp.exp2(ctot3 - w3).reshape(tile, d)) * k_sign
    vb_scr[...] = y_scr[:, 2 * d:3 * d].astype(bf16)
    g = y_scr[:, 3 * d:4 * d]
    sg_scr[...] = g * _sigmoid(g)
    in_proj(hnext_ref)

    for h in range(d // V7X_LANES):
        ctmp_scr[h] = c_scr[:, h * V7X_LANES:(h + 1) * V7X_LANES]
    ctot = jnp.concatenate([ctmp_scr[h, pl.ds(SUB - 1, nb, stride=SUB), :]
                            for h in range(d // V7X_LANES)], axis=1)
    ctot_scr[...] = ctot
    ginc_scr[...] = _block_cumsum(ctot.reshape(n_chunks, BLOCKS_PER_CHUNK, d)).reshape(nb, d)

    sub = lax.broadcasted_iota(jnp.int32, (nb, SUB, d), 1)
    for s in range(SUB):
        c3 = c_scr[...].reshape(nb, SUB, d)
        w_s = w_scr[...].reshape(nb, SUB, d)[:, s:s + 1, :]
        p = q_scr[...].reshape(nb, SUB, d) * jnp.exp2(jnp.where(sub >= s, c3 - w_s, MASK_VALUE))
        p = p.reshape(tile, d).astype(bf16)
        for m in range(n_pairs):
            lhs_scr[m * tile:(m + 1) * tile, s * PAIR:(s + 1) * PAIR] = p[:, m * PAIR:(m + 1) * PAIR]
    half = n_pairs * tile // 2
    diag_scr[0:half] = jnp.dot(lhs_scr[0:half], wsel_ref[...], preferred_element_type=f32)
    diag_scr[half:] = jnp.dot(lhs_scr[half:], wselw_ref[...],
                              preferred_element_type=f32)[:, :2 * CHUNK]

    lane = lax.broadcasted_iota(jnp.int32, (SUB, V7X_LANES), 1)
    row = lax.broadcasted_iota(jnp.int32, (SUB, V7X_LANES), 0)
    col_block = (lane % CHUNK) // SUB
    col_sub = lane % SUB

    units = [(ci, m) for ci in range(n_chunks) for m in range(n_pairs)]
    per_chunk = []
    for ci in range(n_chunks):
        r0, b0 = ci * CHUNK, ci * BLOCKS_PER_CHUNK
        qt = qt_scr[r0:r0 + CHUNK, :]
        kt = kt_scr[r0:r0 + CHUNK, :]
        ginc = ginc_scr[b0:b0 + BLOCKS_PER_CHUNK, :]
        gexc = ginc - ctot_scr[b0:b0 + BLOCKS_PER_CHUNK, :]
        btot = ginc[BLOCKS_PER_CHUNK - 1:BLOCKS_PER_CHUNK, :]
        e_g = jnp.exp2(gexc)
        e_r = jnp.exp2(btot - ginc)
        qh = jnp.concatenate([qt[SUB * i:SUB * (i + 1)] * e_g[i:i + 1] for i in range(BLOCKS_PER_CHUNK)],
                             axis=0).astype(bf16)
        kh = jnp.concatenate([kt[SUB * i:SUB * (i + 1)] * e_r[i:i + 1] for i in range(BLOCKS_PER_CHUNK)],
                             axis=0).astype(bf16)
        rows = []
        for i in range(1, BLOCKS_PER_CHUNK):
            d_i = jnp.exp2(gexc[i:i + 1] - ginc[0:i])
            for j in range(i):
                rows.append(qt[SUB * i:SUB * (i + 1)] * d_i[j:j + 1])
        lhs_off = jnp.concatenate(rows, axis=0).astype(bf16)
        per_chunk.append(dict(qh=qh, kh=kh, lhs_off=lhs_off, ktb=kt.astype(bf16),
                              vb=vb_scr[r0:r0 + CHUNK, :], e_b=jnp.exp2(btot)))

    def pair_lanes(m):
        return m * PAIR, m * PAIR + A_KEY_DIM, (m + 1) * PAIR

    offs = {}
    for ci, m in units:
        lo, mid, hi = pair_lanes(m)
        c = per_chunk[ci]
        k_bd = _block_diag2(c["ktb"][:, lo:mid], c["ktb"][:, mid:hi])
        offs[ci, m] = lax.dot_general(c["lhs_off"][:, lo:hi], k_bd, (((1,), (1,)), ((), ())),
                                      preferred_element_type=f32)
    a2s = {}
    for ci, m in units:
        off = offs[ci, m]
        dg = diag_scr[m * tile + ci * CHUNK:m * tile + (ci + 1) * CHUNK, :]
        blocks = []
        for i in range(BLOCKS_PER_CHUNK):
            a = jnp.where(col_block == i, jnp.where(col_sub <= row, dg[SUB * i:SUB * (i + 1)], 0.0), 0.0)
            for j in range(i):
                pidx = i * (i - 1) // 2 + j
                a = jnp.where(col_block == j, off[SUB * pidx:SUB * (pidx + 1)], a)
            blocks.append(a)
        a2s[ci, m] = jnp.concatenate(blocks, axis=0).astype(bf16)
    for ci, m in units:
        lo, mid, hi = pair_lanes(m)
        vb = per_chunk[ci]["vb"]
        v_bd = _block_diag2(vb[:, lo:mid], vb[:, mid:hi])
        o_scr[ci * CHUNK:(ci + 1) * CHUNK, lo:hi] = jnp.dot(a2s[ci, m], v_bd,
                                                            preferred_element_type=f32)

    for ci in range(n_chunks):
        c = per_chunk[ci]
        for m in range(n_pairs):
            lo, mid, hi = pair_lanes(m)
            st_e = st_ref[2 * m]
            st_o = st_ref[2 * m + 1]
            st_bd = _block_diag2(st_e.astype(bf16), st_o.astype(bf16))
            o_inter = lax.dot_general(c["qh"][:, lo:hi], st_bd, (((1,), (1,)), ((), ())),
                                      preferred_element_type=f32)
            o_scr[ci * CHUNK:(ci + 1) * CHUNK, lo:hi] += o_inter
            upd = lax.dot_general(c["vb"][:, lo:hi], c["kh"][:, lo:hi], (((0,), (0,)), ((), ())),
                                  preferred_element_type=f32)
            st_ref[2 * m] = c["e_b"][:, lo:mid] * st_e + upd[:A_KEY_DIM, :A_KEY_DIM]
            st_ref[2 * m + 1] = c["e_b"][:, mid:hi] * st_o + upd[A_KEY_DIM:, A_KEY_DIM:]

    o = o_scr[...]
    osq = (o * o).astype(bf16)
    ms = jnp.concatenate(
        [jnp.dot(osq[:, m * PAIR:(m + 1) * PAIR], ones_ref[...], preferred_element_type=f32)
         for m in range(n_pairs)], axis=1) * (1.0 / A_KEY_DIM)
    out = o * lax.rsqrt(ms + NORM_EPS) * ogain_ref[...]
    out = (out * sg_scr[...]).astype(bf16)
    mix = jnp.dot(out, wout_ref[...], preferred_element_type=f32)
    o_ref[...] = h_ref[...] + gate_ref[...] * mix


def _hgrn_selector():
    r = np.arange(SUB * PAIR)
    s_local, h_r = r // PAIR, (r % PAIR) // A_KEY_DIM
    c = np.arange(2 * CHUNK)
    h_c, s_c = c // CHUNK, c % CHUNK
    sel = (h_r[:, None] == h_c[None, :]) & (s_local[:, None] == (s_c % SUB)[None, :])
    return jnp.asarray(sel, dtype=bf16)


def _hgrn_layer(h, gain, mods, layer, w_in, w_out, a_lb, o_gain):
    bsz, seq, d = h.shape
    assert d == A_HEADS * A_KEY_DIM
    tile = min(HGRN_TILE, seq)
    n_pairs = A_HEADS // 2
    n_in = w_in.shape[-1]
    ones_bd = jnp.asarray(np.kron(np.eye(2), np.ones((A_KEY_DIM, A_KEY_DIM))), dtype=bf16)
    gain_t = jnp.tile(o_gain.astype(f32), A_HEADS).reshape(1, d)
    n_a = a_lb.shape[0]
    sel = _hgrn_selector()
    h_spec = pl.BlockSpec((None, tile, d), lambda b, i: (b, i, 0))
    return pl.pallas_call(
        functools.partial(_hgrn_layer_kernel, layer=layer, tile=tile),
        grid=(bsz, seq // tile),
        in_specs=[
            h_spec,
            pl.BlockSpec((None, tile, d), lambda b, i: (b, jnp.minimum(i + 1, seq // tile - 1), 0)),
            _row_spec(d), _mod_spec(layer, 0, d), _mod_spec(layer, 1, d),
            _mod_spec(layer, 2, d),
            _resident((None, d, n_in), lambda b, i: (layer, 0, 0)),
            _resident((None, d, d), lambda b, i: (layer, 0, 0)),
            pl.BlockSpec((n_a, d), lambda b, i: (0, 0)),
            _row_spec(d),
            _resident((SUB * PAIR, 2 * CHUNK), lambda b, i: (0, 0)),
            _resident((SUB * PAIR, 4 * CHUNK), lambda b, i: (0, 0)),
            _resident((PAIR, PAIR), lambda b, i: (0, 0)),
        ],
        out_specs=h_spec,
        out_shape=jax.ShapeDtypeStruct((bsz, seq, d), f32),
        scratch_shapes=[
            pltpu.VMEM((A_HEADS, A_KEY_DIM, A_KEY_DIM), f32),
            pltpu.VMEM((tile, n_in), f32),
            pltpu.VMEM((tile, d), f32),
            pltpu.VMEM((tile, d), f32),
            pltpu.VMEM((tile, d), f32),
            pltpu.VMEM((tile, d), f32),
            pltpu.VMEM((tile, d), f32),
            pltpu.VMEM((tile, d), bf16),
            pltpu.VMEM((tile, d), f32),
            pltpu.VMEM((d // V7X_LANES, tile, V7X_LANES), f32),
            pltpu.VMEM((tile // SUB, d), f32),
            pltpu.VMEM((tile // SUB, d), f32),
            pltpu.VMEM((n_pairs * tile, SUB * PAIR), bf16),
            pltpu.VMEM((n_pairs * tile, 2 * CHUNK), f32),
            pltpu.VMEM((tile, d), f32),
        ],
        compiler_params=_params("parallel", "arbitrary"),
        name="hgrn_layer",
    )(h, h, gain.reshape(1, d), mods, mods, mods, w_in, w_out, a_lb.astype(f32), gain_t,
      sel, jnp.concatenate([sel, jnp.zeros_like(sel)], axis=1), ones_bd)


def _attn_layer_kernel(h_ref, hnext_ref, gain_ref, shift_ref, scale_ref, gate_ref, wq_ref, wo_ref,
                       k0_ref, k1_ref, k2_ref, vt0_ref, vt1_ref, vt2_ref, bias_ref, o_ref,
                       q_scr, qnext_scr, biasp_scr, ot_scr):
    i = pl.program_id(1)

    def q_proj(src_ref):
        u = _norm_mod(src_ref[...], gain_ref[...], shift_ref[...], scale_ref[...]).astype(bf16)
        q = jnp.dot(u, wq_ref[...], preferred_element_type=f32) * (B_HEAD_DIM ** -0.5 * LOG2_E)
        qnext_scr[...] = q.astype(bf16)

    @pl.when(i == 0)
    def _():
        q_proj(h_ref)

    @pl.when(i <= ATT_WIN // ATT_TILE - 1)
    def _():
        krow = lax.broadcasted_iota(jnp.int32, (ATT_WIN, ATT_TILE), 0)
        pad = jnp.where(krow >= (ATT_WIN // ATT_TILE - 1 - i) * ATT_TILE, 0.0, MASK_VALUE)
        for hd in range(B_HEADS):
            biasp_scr[hd] = bias_ref[hd] + pad

    q_scr[...] = qnext_scr[...]
    q_proj(hnext_ref)

    lane = lax.broadcasted_iota(jnp.int32, (1, ATT_GROUP_LANES), 1)
    head_of_lane = lane // B_HEAD_DIM
    ones_rows = jnp.ones((ATT_ONES_ROWS, ATT_WIN), bf16)
    for h0 in range(0, B_HEADS, ATT_STAGE_HEADS):
        heads = list(range(h0, h0 + ATT_STAGE_HEADS))
        s_ts = []
        for head in heads:
            g, hh = divmod(head, ATT_GROUP)
            lanes = slice(g * ATT_GROUP_LANES, (g + 1) * ATT_GROUP_LANES)
            kwin = jnp.concatenate([k0_ref[:, lanes], k1_ref[:, lanes], k2_ref[:, lanes]], axis=0)
            keep = jnp.where(head_of_lane == hh, 1.0, 0.0).astype(bf16)
            s_ts.append(lax.dot_general(kwin, q_scr[:, lanes] * keep, (((1,), (1,)), ((), ())),
                                        preferred_element_type=f32))
        p_ts = []
        for head, s_t in zip(heads, s_ts):
            s_t = s_t + biasp_scr[head]
            p_ts.append(jnp.exp2(s_t - jnp.max(s_t, axis=0, keepdims=True)).astype(bf16))
        for head, p_t in zip(heads, p_ts):
            rows = slice(head * B_HEAD_DIM, (head + 1) * B_HEAD_DIM)
            v_t = jnp.concatenate([vt0_ref[rows, :], vt1_ref[rows, :], vt2_ref[rows, :]], axis=1)
            o_t = jnp.dot(jnp.concatenate([v_t, ones_rows], axis=0), p_t,
                          preferred_element_type=f32)
            ot_scr[rows, :] = o_t[:B_HEAD_DIM] * (1.0 / o_t[B_HEAD_DIM:B_HEAD_DIM + 1])
    mix = lax.dot_general(ot_scr[...].astype(bf16), wo_ref[...], (((0,), (0,)), ((), ())),
                          preferred_element_type=f32)
    o_ref[...] = h_ref[...] + gate_ref[...] * mix


def _attn_bias_table(rel_bias):
    n_rel, n_heads = rel_bias.shape
    span = ATT_TILE + ATT_WIN
    j = np.arange(span)
    delta = np.where(j < ATT_WIN, j, j - span)
    idx = np.clip(delta - B_PAST_CHUNKS * CHUNK, -REL_CLIP, CHUNK - 1) + REL_CLIP
    e = (rel_bias.astype(f32) * LOG2_E)[jnp.asarray(idx)].T
    flat = jnp.tile(e, (1, ATT_TILE))[:, :ATT_TILE * (span - 1)]
    table = flat.reshape(n_heads, ATT_TILE, span - 1)[:, :, :ATT_WIN]
    qchunk = np.arange(ATT_TILE)[:, None] // CHUNK + B_PAST_CHUNKS
    back = qchunk - np.arange(ATT_WIN)[None, :] // CHUNK
    in_band = (back >= 0) & (back <= B_PAST_CHUNKS)
    return jnp.swapaxes(jnp.where(jnp.asarray(in_band)[None], table, MASK_VALUE), 1, 2)


def _attn_layer(h, gain, mods, layer, w_q, w_o, w_layer, k, v_t, rel_bias):
    bsz, seq, d = h.shape
    assert seq % ATT_TILE == 0 and d == B_HEADS * B_HEAD_DIM
    bias = _attn_bias_table(rel_bias)
    h_spec = pl.BlockSpec((None, ATT_TILE, d), lambda b, i: (b, i, 0))

    def k_spec(back):
        return pl.BlockSpec((None, ATT_TILE, d), lambda b, i: (b, jnp.maximum(i - back, 0), 0))

    def vt_spec(back):
        return pl.BlockSpec((None, d, ATT_TILE), lambda b, i: (b, 0, jnp.maximum(i - back, 0)))

    return pl.pallas_call(
        _attn_layer_kernel,
        grid=(bsz, seq // ATT_TILE),
        in_specs=[
            h_spec,
            pl.BlockSpec((None, ATT_TILE, d), lambda b, i: (b, jnp.minimum(i + 1, seq // ATT_TILE - 1), 0)),
            _row_spec(d), _mod_spec(layer, 0, d), _mod_spec(layer, 1, d),
            _mod_spec(layer, 2, d),
            _resident((None, d, d), lambda b, i: (w_layer, 0, 0)),
            _resident((None, d, d), lambda b, i: (w_layer, 0, 0)),
            k_spec(2), k_spec(1), k_spec(0),
            vt_spec(2), vt_spec(1), vt_spec(0),
            _resident((B_HEADS, ATT_WIN, ATT_TILE), lambda b, i: (0, 0, 0)),
        ],
        out_specs=h_spec,
        out_shape=jax.ShapeDtypeStruct((bsz, seq, d), f32),
        scratch_shapes=[
            pltpu.VMEM((ATT_TILE, d), bf16),
            pltpu.VMEM((ATT_TILE, d), bf16),
            pltpu.VMEM((B_HEADS, ATT_WIN, ATT_TILE), f32),
            pltpu.VMEM((d, ATT_TILE), f32),
        ],
        compiler_params=_params("parallel", "arbitrary"),
        name="attn_layer",
    )(h, h, gain.reshape(1, d), mods, mods, mods, w_q, w_o, k, k, k, v_t, v_t, v_t, bias)


def kernel(x, c, mod_w, mod_b, norm_mix, norm_ffn, ffn_w_in, ffn_w_out, a_w_in, a_w_out, a_lb,
           a_out_norm, kv_norm, kv_mod_w, kv_mod_b, kv_w, b_w_q, b_w_o, b_rel_bias, final_norm):
    depth = mod_w.shape[0]
    n_a = a_w_in.shape[0]
    bsz = x.shape[0]

    mods = _modulation(c, mod_w, mod_b).reshape(depth, bsz, 1, -1)
    kv_mods = _modulation(c, kv_mod_w[None], kv_mod_b[None]).reshape(1, bsz, 1, -1)

    ffn_w_in_b = ffn_w_in.astype(bf16)
    ffn_w_out_b = ffn_w_out.astype(bf16)
    a_w_in_b = a_w_in.astype(bf16)
    a_w_out_b = a_w_out.astype(bf16)
    kv_w_b = kv_w.astype(bf16)
    b_w_q_b = b_w_q.astype(bf16)
    b_w_o_b = b_w_o.astype(bf16)

    h = x
    k = v_t = None
    for layer in range(depth):
        if layer < n_a:
            h = _hgrn_layer(h, norm_mix[layer], mods, layer, a_w_in_b, a_w_out_b, a_lb,
                            a_out_norm[layer])
        else:
            j = layer - n_a
            h = _attn_layer(h, norm_mix[layer], mods, layer, b_w_q_b, b_w_o_b, j, k, v_t,
                            b_rel_bias[j])
        if layer == n_a - 1:
            h, k, v_t = _ffn(h, norm_ffn[layer], mods, layer, ffn_w_in_b, ffn_w_out_b,
                         kv=(kv_norm, kv_mods, kv_w_b))
        elif layer == depth - 1:
            h = _ffn(h, norm_ffn[layer], mods, layer, ffn_w_in_b, ffn_w_out_b, final_gain=final_norm)
        else:
            h = _ffn(h, norm_ffn[layer], mods, layer, ffn_w_in_b, ffn_w_out_b)
    return h
```

```python
import functools

import numpy as np
import jax
import jax.numpy as jnp
from jax import lax
from jax.experimental import pallas as pl
from jax.experimental.pallas import tpu as pltpu

V7X_LANES = 128
V7X_SUBLANES = 8
V7X_MXU_DIM = 256
V7X_VMEM_LIMIT_BYTES = 56 * 1024 * 1024

CHUNK = 64
A_HEADS = 8
A_KEY_DIM = 128
B_HEADS = 16
B_HEAD_DIM = 64
B_PAST_CHUNKS = 8
REL_CLIP = 256
NORM_EPS = 1e-6
N_MOD = 6
MASK_VALUE = -1e30
MIN_FORGET = 1e-30
LOG2_E = 1.4426950408889634

SUB = V7X_SUBLANES
BLOCKS_PER_CHUNK = CHUNK // SUB
PAIR = 2 * A_KEY_DIM
N_OFF_PAIRS = BLOCKS_PER_CHUNK * (BLOCKS_PER_CHUNK - 1) // 2

HGRN_TILE = 256
ATT_TILE = 256
ATT_WIN = ATT_TILE + B_PAST_CHUNKS * CHUNK
ATT_GROUP = 4
ATT_GROUP_LANES = ATT_GROUP * B_HEAD_DIM
ATT_STAGE_HEADS = 8
ATT_ONES_ROWS = 16
ROW_TILE = 512
MOD_MAX_TILE = 2048

f32 = jnp.float32
bf16 = jnp.bfloat16


def _params(*sem):
    return pltpu.CompilerParams(dimension_semantics=sem, vmem_limit_bytes=V7X_VMEM_LIMIT_BYTES)


def _resident(shape, index_map):
    return pl.BlockSpec(shape, index_map, pipeline_mode=pl.Buffered(1))


def _row_spec(d):
    return pl.BlockSpec((1, d), lambda b, i: (0, 0))


def _mod_spec(layer, col, d):
    return pl.BlockSpec((None, None, 1, d), lambda b, i: (layer, b, 0, col))


def _sigmoid(x):
    return 1.0 / (1.0 + jnp.exp(-x))


def _norm_mod(x, gain, shift, scale):
    y = x * lax.rsqrt(jnp.mean(x * x, axis=-1, keepdims=True) + NORM_EPS)
    return (y * gain) * (1.0 + scale) + shift


def _mod_kernel(c_ref, w_ref, b_ref, o_ref):
    c = c_ref[...]
    c_act = c * _sigmoid(c)
    o_ref[...] = jnp.dot(c_act, w_ref[...], preferred_element_type=f32,
                         precision=lax.Precision.HIGHEST) + b_ref[...]


def _modulation(c, w, b):
    n_layers, d, n = w.shape
    bsz = c.shape[0]
    tn = max(t for t in range(V7X_LANES, MOD_MAX_TILE + 1, V7X_LANES) if n % t == 0)
    return pl.pallas_call(
        _mod_kernel,
        grid=(n_layers, n // tn),
        in_specs=[
            pl.BlockSpec((bsz, d), lambda l, j: (0, 0)),
            pl.BlockSpec((None, d, tn), lambda l, j: (l, 0, j)),
            pl.BlockSpec((None, 1, tn), lambda l, j: (l, 0, j)),
        ],
        out_specs=pl.BlockSpec((None, bsz, tn), lambda l, j: (l, 0, j)),
        out_shape=jax.ShapeDtypeStruct((n_layers, bsz, n), f32),
        compiler_params=_params("arbitrary", "arbitrary"),
        name="modulation",
    )(c, w, b.reshape(n_layers, 1, n))


def _ffn_kernel(*refs, ffn_dim, final_norm, with_kv):
    h_ref, gain_ref, shift_ref, scale_ref, gate_ref, win_ref, wout_ref = refs[:7]
    rest = refs[7:]
    x = h_ref[...]
    u = _norm_mod(x, gain_ref[...], shift_ref[...], scale_ref[...]).astype(bf16)
    mid = (ffn_dim // 2) // V7X_MXU_DIM * V7X_MXU_DIM
    acc = None
    for lo, hi in ((0, mid), (mid, ffn_dim)):
        a = jnp.dot(u, win_ref[:, lo:hi], preferred_element_type=f32)
        b = jnp.dot(u, win_ref[:, ffn_dim + lo:ffn_dim + hi], preferred_element_type=f32)
        act = ((a * _sigmoid(a)) * b).astype(bf16)
        p = jnp.dot(act, wout_ref[lo:hi, :], preferred_element_type=f32)
        acc = p if acc is None else acc + p
    out = x + gate_ref[...] * acc
    if with_kv:
        kgain_ref, kshift_ref, kscale_ref, kvw_ref, o_ref, k_ref, vt_ref = rest
        ukv = _norm_mod(out, kgain_ref[...], kshift_ref[...], kscale_ref[...]).astype(bf16)
        kv = jnp.dot(ukv, kvw_ref[...], preferred_element_type=f32)
        d_kv = k_ref.shape[-1]
        k_ref[...] = kv[:, :d_kv].astype(k_ref.dtype)
        v_t = kv[:, d_kv:].T.astype(vt_ref.dtype)
        for j in range(vt_ref.shape[0]):
            vt_ref[j] = v_t[:, j * ATT_TILE:(j + 1) * ATT_TILE]
    elif final_norm:
        fgain_ref, o_ref = rest
        out = out * lax.rsqrt(jnp.mean(out * out, axis=-1, keepdims=True) + NORM_EPS)
        out = out * fgain_ref[...]
    else:
        (o_ref,) = rest
    o_ref[...] = out


def _ffn(h, gain, mods, layer, w_in, w_out, final_gain=None, kv=None):
    bsz, seq, d = h.shape
    ffn_dim = w_out.shape[1]
    tm = min(ROW_TILE, seq)
    h_spec = pl.BlockSpec((None, tm, d), lambda b, i: (b, i, 0))
    in_specs = [
        h_spec, _row_spec(d), _mod_spec(layer, 3, d), _mod_spec(layer, 4, d), _mod_spec(layer, 5, d),
        _resident((None, d, 2 * ffn_dim), lambda b, i: (layer, 0, 0)),
        _resident((None, ffn_dim, d), lambda b, i: (layer, 0, 0)),
    ]
    args = [h, gain.reshape(1, d), mods, mods, mods, w_in, w_out]
    out_specs = h_spec
    out_shape = jax.ShapeDtypeStruct((bsz, seq, d), f32)
    if kv is not None:
        kv_norm, kv_mods, kv_w = kv
        n_kv = kv_w.shape[-1]
        in_specs += [_row_spec(d), _mod_spec(0, 0, d), _mod_spec(0, 1, d),
                     _resident((d, n_kv), lambda b, i: (0, 0))]
        args += [kv_norm.reshape(1, d), kv_mods, kv_mods, kv_w]
        out_specs = (h_spec, pl.BlockSpec((None, tm, n_kv // 2), lambda b, i: (b, i, 0)),
                     pl.BlockSpec((None, tm // ATT_TILE, n_kv // 2, ATT_TILE),
                                  lambda b, i: (b, i, 0, 0)))
        out_shape = (out_shape, jax.ShapeDtypeStruct((bsz, seq, n_kv // 2), bf16),
                     jax.ShapeDtypeStruct((bsz, seq // ATT_TILE, n_kv // 2, ATT_TILE), bf16))
    elif final_gain is not None:
        in_specs += [_row_spec(d)]
        args += [final_gain.reshape(1, d)]
    return pl.pallas_call(
        functools.partial(_ffn_kernel, ffn_dim=ffn_dim, final_norm=final_gain is not None,
                          with_kv=kv is not None),
        grid=(bsz, seq // tm),
        in_specs=in_specs,
        out_specs=out_specs,
        out_shape=out_shape,
        compiler_params=_params("parallel", "arbitrary"),
        name="ffn",
    )(*args)


def _block_diag2(a, b):
    z = jnp.zeros_like(a)
    return jnp.concatenate([jnp.concatenate([a, z], axis=1),
                            jnp.concatenate([z, b], axis=1)], axis=0)


def _block_cumsum(x3):
    sub = lax.broadcasted_iota(jnp.int32, x3.shape, 1)
    for sh in (1, 2, 4):
        x3 = x3 + jnp.where(sub >= sh, pltpu.roll(x3, sh, axis=1), 0.0)
    return x3


def _hgrn_layer_kernel(h_ref, hnext_ref, gain_ref, shift_ref, scale_ref, gate_ref, win_ref, wout_ref,
                       alb_ref, ogain_ref, wsel_ref, wselw_ref, ones_ref, o_ref,
                       st_ref, y_scr, q_scr, w_scr, c_scr, qt_scr, kt_scr, vb_scr, sg_scr, ctmp_scr,
                       ctot_scr, ginc_scr, lhs_scr, diag_scr, o_scr, *, layer, tile):
    d = A_HEADS * A_KEY_DIM
    nb = tile // SUB
    n_chunks = tile // CHUNK
    n_pairs = A_HEADS // 2

    def in_proj(src_ref):
        u = _norm_mod(src_ref[...], gain_ref[...], shift_ref[...], scale_ref[...]).astype(bf16)
        y_scr[...] = jnp.dot(u, win_ref[...], preferred_element_type=f32)

    @pl.when(pl.program_id(1) == 0)
    def _():
        st_ref[...] = jnp.zeros_like(st_ref)
        in_proj(h_ref)

    a_lb = alb_ref[...]
    e_lb = jnp.exp(a_lb - jnp.max(a_lb, axis=0, keepdims=True))
    sm = e_lb / jnp.sum(e_lb, axis=0, keepdims=True)
    lb = jnp.zeros((1, d), f32)
    for m in range(1, layer + 1):
        lb = lb + sm[m:m + 1, :]
    oml = 1.0 - lb

    qpre = y_scr[:, 0:d]
    zf = y_scr[:, d:2 * d]
    q = qpre * _sigmoid(qpre)
    sig = _sigmoid(zf)
    log_f = jnp.log2(jnp.maximum(lb + oml * sig, MIN_FORGET))
    k_sign = jnp.where(oml < 0.0, -1.0, 1.0)
    log_k = jnp.log2(jnp.abs(oml) * (1.0 - sig))

    c3 = _block_cumsum(log_f.reshape(nb, SUB, d))
    ctot3 = c3[:, SUB - 1:SUB, :]
    w3 = c3 - log_k.reshape(nb, SUB, d)
    q_scr[...] = q * k_sign
    c_scr[...] = c3.reshape(tile, d)
    w_scr[...] = w3.reshape(tile, d)
    qt_scr[...] = (q.reshape(nb, SUB, d) * jnp.exp2(c3)).reshape(tile, d)
    kt_scr[...] = (jnp.exp2(ctot3 - w3).reshape(tile, d)) * k_sign
    vb_scr[...] = y_scr[:, 2 * d:3 * d].astype(bf16)
    g = y_scr[:, 3 * d:4 * d]
    sg_scr[...] = g * _sigmoid(g)
    in_proj(hnext_ref)

    for h in range(d // V7X_LANES):
        ctmp_scr[h] = c_scr[:, h * V7X_LANES:(h + 1) * V7X_LANES]
    ctot = jnp.concatenate([ctmp_scr[h, pl.ds(SUB - 1, nb, stride=SUB), :]
                            for h in range(d // V7X_LANES)], axis=1)
    ctot_scr[...] = ctot
    ginc_scr[...] = _block_cumsum(ctot.reshape(n_chunks, BLOCKS_PER_CHUNK, d)).reshape(nb, d)

    sub = lax.broadcasted_iota(jnp.int32, (nb, SUB, d), 1)
    for s in range(SUB):
        c3 = c_scr[...].reshape(nb, SUB, d)
        w_s = w_scr[...].reshape(nb, SUB, d)[:, s:s + 1, :]
        p = q_scr[...].reshape(nb, SUB, d) * jnp.exp2(jnp.where(sub >= s, c3 - w_s, MASK_VALUE))
        p = p.reshape(tile, d).astype(bf16)
        for m in range(n_pairs):
            lhs_scr[m * tile:(m + 1) * tile, s * PAIR:(s + 1) * PAIR] = p[:, m * PAIR:(m + 1) * PAIR]
    half = n_pairs * tile // 2
    diag_scr[0:half] = jnp.dot(lhs_scr[0:half], wsel_ref[...], preferred_element_type=f32)
    diag_scr[half:] = jnp.dot(lhs_scr[half:], wselw_ref[...],
                              preferred_element_type=f32)[:, :2 * CHUNK]

    lane = lax.broadcasted_iota(jnp.int32, (SUB, V7X_LANES), 1)
    row = lax.broadcasted_iota(jnp.int32, (SUB, V7X_LANES), 0)
    col_block = (lane % CHUNK) // SUB
    col_sub = lane % SUB

    units = [(ci, m) for ci in range(n_chunks) for m in range(n_pairs)]
    per_chunk = []
    for ci in range(n_chunks):
        r0, b0 = ci * CHUNK, ci * BLOCKS_PER_CHUNK
        qt = qt_scr[r0:r0 + CHUNK, :]
        kt = kt_scr[r0:r0 + CHUNK, :]
        ginc = ginc_scr[b0:b0 + BLOCKS_PER_CHUNK, :]
        gexc = ginc - ctot_scr[b0:b0 + BLOCKS_PER_CHUNK, :]
        btot = ginc[BLOCKS_PER_CHUNK - 1:BLOCKS_PER_CHUNK, :]
        e_g = jnp.exp2(gexc)
        e_r = jnp.exp2(btot - ginc)
        qh = jnp.concatenate([qt[SUB * i:SUB * (i + 1)] * e_g[i:i + 1] for i in range(BLOCKS_PER_CHUNK)],
                             axis=0).astype(bf16)
        kh = jnp.concatenate([kt[SUB * i:SUB * (i + 1)] * e_r[i:i + 1] for i in range(BLOCKS_PER_CHUNK)],
                             axis=0).astype(bf16)
        rows = []
        for i in range(1, BLOCKS_PER_CHUNK):
            d_i = jnp.exp2(gexc[i:i + 1] - ginc[0:i])
            for j in range(i):
                rows.append(qt[SUB * i:SUB * (i + 1)] * d_i[j:j + 1])
        lhs_off = jnp.concatenate(rows, axis=0).astype(bf16)
        per_chunk.append(dict(qh=qh, kh=kh, lhs_off=lhs_off, ktb=kt.astype(bf16),
                              vb=vb_scr[r0:r0 + CHUNK, :], e_b=jnp.exp2(btot)))

    def pair_lanes(m):
        return m * PAIR, m * PAIR + A_KEY_DIM, (m + 1) * PAIR

    offs = {}
    for ci, m in units:
        lo, mid, hi = pair_lanes(m)
        c = per_chunk[ci]
        k_bd = _block_diag2(c["ktb"][:, lo:mid], c["ktb"][:, mid:hi])
        offs[ci, m] = lax.dot_general(c["lhs_off"][:, lo:hi], k_bd, (((1,), (1,)), ((), ())),
                                      preferred_element_type=f32)
    a2s = {}
    for ci, m in units:
        off = offs[ci, m]
        dg = diag_scr[m * tile + ci * CHUNK:m * tile + (ci + 1) * CHUNK, :]
        blocks = []
        for i in range(BLOCKS_PER_CHUNK):
            a = jnp.where(col_block == i, jnp.where(col_sub <= row, dg[SUB * i:SUB * (i + 1)], 0.0), 0.0)
            for j in range(i):
                pidx = i * (i - 1) // 2 + j
                a = jnp.where(col_block == j, off[SUB * pidx:SUB * (pidx + 1)], a)
            blocks.append(a)
        a2s[ci, m] = jnp.concatenate(blocks, axis=0).astype(bf16)
    for ci, m in units:
        lo, mid, hi = pair_lanes(m)
        vb = per_chunk[ci]["vb"]
        v_bd = _block_diag2(vb[:, lo:mid], vb[:, mid:hi])
        o_scr[ci * CHUNK:(ci + 1) * CHUNK, lo:hi] = jnp.dot(a2s[ci, m], v_bd,
                                                            preferred_element_type=f32)

    for ci in range(n_chunks):
        c = per_chunk[ci]
        for m in range(n_pairs):
            lo, mid, hi = pair_lanes(m)
            st_e = st_ref[2 * m]
            st_o = st_ref[2 * m + 1]
            st_bd = _block_diag2(st_e.astype(bf16), st_o.astype(bf16))
            o_inter = lax.dot_general(c["qh"][:, lo:hi], st_bd, (((1,), (1,)), ((), ())),
                                      preferred_element_type=f32)
            o_scr[ci * CHUNK:(ci + 1) * CHUNK, lo:hi] += o_inter
            upd = lax.dot_general(c["vb"][:, lo:hi], c["kh"][:, lo:hi], (((0,), (0,)), ((), ())),
                                  preferred_element_type=f32)
            st_ref[2 * m] = c["e_b"][:, lo:mid] * st_e + upd[:A_KEY_DIM, :A_KEY_DIM]
            st_ref[2 * m + 1] = c["e_b"][:, mid:hi] * st_o + upd[A_KEY_DIM:, A_KEY_DIM:]

    o = o_scr[...]
    osq = (o * o).astype(bf16)
    ms = jnp.concatenate(
        [jnp.dot(osq[:, m * PAIR:(m + 1) * PAIR], ones_ref[...], preferred_element_type=f32)
         for m in range(n_pairs)], axis=1) * (1.0 / A_KEY_DIM)
    out = o * lax.rsqrt(ms + NORM_EPS) * ogain_ref[...]
    out = (out * sg_scr[...]).astype(bf16)
    mix = jnp.dot(out, wout_ref[...], preferred_element_type=f32)
    o_ref[...] = h_ref[...] + gate_ref[...] * mix


def _hgrn_selector():
    r = np.arange(SUB * PAIR)
    s_local, h_r = r // PAIR, (r % PAIR) // A_KEY_DIM
    c = np.arange(2 * CHUNK)
    h_c, s_c = c // CHUNK, c % CHUNK
    sel = (h_r[:, None] == h_c[None, :]) & (s_local[:, None] == (s_c % SUB)[None, :])
    return jnp.asarray(sel, dtype=bf16)


def _hgrn_layer(h, gain, mods, layer, w_in, w_out, a_lb, o_gain):
    bsz, seq, d = h.shape
    assert d == A_HEADS * A_KEY_DIM
    tile = min(HGRN_TILE, seq)
    n_pairs = A_HEADS // 2
    n_in = w_in.shape[-1]
    ones_bd = jnp.asarray(np.kron(np.eye(2), np.ones((A_KEY_DIM, A_KEY_DIM))), dtype=bf16)
    gain_t = jnp.tile(o_gain.astype(f32), A_HEADS).reshape(1, d)
    n_a = a_lb.shape[0]
    sel = _hgrn_selector()
    h_spec = pl.BlockSpec((None, tile, d), lambda b, i: (b, i, 0))
    return pl.pallas_call(
        functools.partial(_hgrn_layer_kernel, layer=layer, tile=tile),
        grid=(bsz, seq // tile),
        in_specs=[
            h_spec,
            pl.BlockSpec((None, tile, d), lambda b, i: (b, jnp.minimum(i + 1, seq // tile - 1), 0)),
            _row_spec(d), _mod_spec(layer, 0, d), _mod_spec(layer, 1, d),
            _mod_spec(layer, 2, d),
            _resident((None, d, n_in), lambda b, i: (layer, 0, 0)),
            _resident((None, d, d), lambda b, i: (layer, 0, 0)),
            pl.BlockSpec((n_a, d), lambda b, i: (0, 0)),
            _row_spec(d),
            _resident((SUB * PAIR, 2 * CHUNK), lambda b, i: (0, 0)),
            _resident((SUB * PAIR, 4 * CHUNK), lambda b, i: (0, 0)),
            _resident((PAIR, PAIR), lambda b, i: (0, 0)),
        ],
        out_specs=h_spec,
        out_shape=jax.ShapeDtypeStruct((bsz, seq, d), f32),
        scratch_shapes=[
            pltpu.VMEM((A_HEADS, A_KEY_DIM, A_KEY_DIM), f32),
            pltpu.VMEM((tile, n_in), f32),
            pltpu.VMEM((tile, d), f32),
            pltpu.VMEM((tile, d), f32),
            pltpu.VMEM((tile, d), f32),
            pltpu.VMEM((tile, d), f32),
            pltpu.VMEM((tile, d), f32),
            pltpu.VMEM((tile, d), bf16),
            pltpu.VMEM((tile, d), f32),
            pltpu.VMEM((d // V7X_LANES, tile, V7X_LANES), f32),
            pltpu.VMEM((tile // SUB, d), f32),
            pltpu.VMEM((tile // SUB, d), f32),
            pltpu.VMEM((n_pairs * tile, SUB * PAIR), bf16),
            pltpu.VMEM((n_pairs * tile, 2 * CHUNK), f32),
            pltpu.VMEM((tile, d), f32),
        ],
        compiler_params=_params("parallel", "arbitrary"),
        name="hgrn_layer",
    )(h, h, gain.reshape(1, d), mods, mods, mods, w_in, w_out, a_lb.astype(f32), gain_t,
      sel, jnp.concatenate([sel, jnp.zeros_like(sel)], axis=1), ones_bd)


def _attn_layer_kernel(h_ref, hnext_ref, gain_ref, shift_ref, scale_ref, gate_ref, wq_ref, wo_ref,
                       k0_ref, k1_ref, k2_ref, vt0_ref, vt1_ref, vt2_ref, bias_ref, o_ref,
                       q_scr, qnext_scr, biasp_scr, ot_scr):
    i = pl.program_id(1)

    def q_proj(src_ref):
        u = _norm_mod(src_ref[...], gain_ref[...], shift_ref[...], scale_ref[...]).astype(bf16)
        q = jnp.dot(u, wq_ref[...], preferred_element_type=f32) * (B_HEAD_DIM ** -0.5 * LOG2_E)
        qnext_scr[...] = q.astype(bf16)

    @pl.when(i == 0)
    def _():
        q_proj(h_ref)

    @pl.when(i <= ATT_WIN // ATT_TILE - 1)
    def _():
        krow = lax.broadcasted_iota(jnp.int32, (ATT_WIN, ATT_TILE), 0)
        pad = jnp.where(krow >= (ATT_WIN // ATT_TILE - 1 - i) * ATT_TILE, 0.0, MASK_VALUE)
        for hd in range(B_HEADS):
            biasp_scr[hd] = bias_ref[hd] + pad

    q_scr[...] = qnext_scr[...]
    q_proj(hnext_ref)

    lane = lax.broadcasted_iota(jnp.int32, (1, ATT_GROUP_LANES), 1)
    head_of_lane = lane // B_HEAD_DIM
    ones_rows = jnp.ones((ATT_ONES_ROWS, ATT_WIN), bf16)
    for h0 in range(0, B_HEADS, ATT_STAGE_HEADS):
        heads = list(range(h0, h0 + ATT_STAGE_HEADS))
        s_ts = []
        for head in heads:
            g, hh = divmod(head, ATT_GROUP)
            lanes = slice(g * ATT_GROUP_LANES, (g + 1) * ATT_GROUP_LANES)
            kwin = jnp.concatenate([k0_ref[:, lanes], k1_ref[:, lanes], k2_ref[:, lanes]], axis=0)
            keep = jnp.where(head_of_lane == hh, 1.0, 0.0).astype(bf16)
            s_ts.append(lax.dot_general(kwin, q_scr[:, lanes] * keep, (((1,), (1,)), ((), ())),
                                        preferred_element_type=f32))
        p_ts = []
        for head, s_t in zip(heads, s_ts):
            s_t = s_t + biasp_scr[head]
            p_ts.append(jnp.exp2(s_t - jnp.max(s_t, axis=0, keepdims=True)).astype(bf16))
        for head, p_t in zip(heads, p_ts):
            rows = slice(head * B_HEAD_DIM, (head + 1) * B_HEAD_DIM)
            v_t = jnp.concatenate([vt0_ref[rows, :], vt1_ref[rows, :], vt2_ref[rows, :]], axis=1)
            o_t = jnp.dot(jnp.concatenate([v_t, ones_rows], axis=0), p_t,
                          preferred_element_type=f32)
            ot_scr[rows, :] = o_t[:B_HEAD_DIM] * (1.0 / o_t[B_HEAD_DIM:B_HEAD_DIM + 1])
    mix = lax.dot_general(ot_scr[...].astype(bf16), wo_ref[...], (((0,), (0,)), ((), ())),
                          preferred_element_type=f32)
    o_ref[...] = h_ref[...] + gate_ref[...] * mix


def _attn_bias_table(rel_bias):
    n_rel, n_heads = rel_bias.shape
    span = ATT_TILE + ATT_WIN
    j = np.arange(span)
    delta = np.where(j < ATT_WIN, j, j - span)
    idx = np.clip(delta - B_PAST_CHUNKS * CHUNK, -REL_CLIP, CHUNK - 1) + REL_CLIP
    e = (rel_bias.astype(f32) * LOG2_E)[jnp.asarray(idx)].T
    flat = jnp.tile(e, (1, ATT_TILE))[:, :ATT_TILE * (span - 1)]
    table = flat.reshape(n_heads, ATT_TILE, span - 1)[:, :, :ATT_WIN]
    qchunk = np.arange(ATT_TILE)[:, None] // CHUNK + B_PAST_CHUNKS
    back = qchunk - np.arange(ATT_WIN)[None, :] // CHUNK
    in_band = (back >= 0) & (back <= B_PAST_CHUNKS)
    return jnp.swapaxes(jnp.where(jnp.asarray(in_band)[None], table, MASK_VALUE), 1, 2)


def _attn_layer(h, gain, mods, layer, w_q, w_o, w_layer, k, v_t, rel_bias):
    bsz, seq, d = h.shape
    assert seq % ATT_TILE == 0 and d == B_HEADS * B_HEAD_DIM
    bias = _attn_bias_table(rel_bias)
    h_spec = pl.BlockSpec((None, ATT_TILE, d), lambda b, i: (b, i, 0))

    def k_spec(back):
        return pl.BlockSpec((None, ATT_TILE, d), lambda b, i: (b, jnp.maximum(i - back, 0), 0))

    def vt_spec(back):
        return pl.BlockSpec((None, None, d, ATT_TILE),
                            lambda b, i: (b, jnp.maximum(i - back, 0), 0, 0))

    return pl.pallas_call(
        _attn_layer_kernel,
        grid=(bsz, seq // ATT_TILE),
        in_specs=[
            h_spec,
            pl.BlockSpec((None, ATT_TILE, d), lambda b, i: (b, jnp.minimum(i + 1, seq // ATT_TILE - 1), 0)),
            _row_spec(d), _mod_spec(layer, 0, d), _mod_spec(layer, 1, d),
            _mod_spec(layer, 2, d),
            _resident((None, d, d), lambda b, i: (w_layer, 0, 0)),
            _resident((None, d, d), lambda b, i: (w_layer, 0, 0)),
            k_spec(2), k_spec(1), k_spec(0),
            vt_spec(2), vt_spec(1), vt_spec(0),
            _resident((B_HEADS, ATT_WIN, ATT_TILE), lambda b, i: (0, 0, 0)),
        ],
        out_specs=h_spec,
        out_shape=jax.ShapeDtypeStruct((bsz, seq, d), f32),
        scratch_shapes=[
            pltpu.VMEM((ATT_TILE, d), bf16),
            pltpu.VMEM((ATT_TILE, d), bf16),
            pltpu.VMEM((B_HEADS, ATT_WIN, ATT_TILE), f32),
            pltpu.VMEM((d, ATT_TILE), f32),
        ],
        compiler_params=_params("parallel", "arbitrary"),
        name="attn_layer",
    )(h, h, gain.reshape(1, d), mods, mods, mods, w_q, w_o, k, k, k, v_t, v_t, v_t, bias)


def kernel(x, c, mod_w, mod_b, norm_mix, norm_ffn, ffn_w_in, ffn_w_out, a_w_in, a_w_out, a_lb,
           a_out_norm, kv_norm, kv_mod_w, kv_mod_b, kv_w, b_w_q, b_w_o, b_rel_bias, final_norm):
    depth = mod_w.shape[0]
    n_a = a_w_in.shape[0]
    bsz = x.shape[0]

    mods = _modulation(c, mod_w, mod_b).reshape(depth, bsz, 1, -1)
    kv_mods = _modulation(c, kv_mod_w[None], kv_mod_b[None]).reshape(1, bsz, 1, -1)

    ffn_w_in_b = ffn_w_in.astype(bf16)
    ffn_w_out_b = ffn_w_out.astype(bf16)
    a_w_in_b = a_w_in.astype(bf16)
    a_w_out_b = a_w_out.astype(bf16)
    kv_w_b = kv_w.astype(bf16)
    b_w_q_b = b_w_q.astype(bf16)
    b_w_o_b = b_w_o.astype(bf16)

    h = x
    k = v_t = None
    for layer in range(depth):
        if layer < n_a:
            h = _hgrn_layer(h, norm_mix[layer], mods, layer, a_w_in_b, a_w_out_b, a_lb,
                            a_out_norm[layer])
        else:
            j = layer - n_a
            h = _attn_layer(h, norm_mix[layer], mods, layer, b_w_q_b, b_w_o_b, j, k, v_t,
                            b_rel_bias[j])
        if layer == n_a - 1:
            h, k, v_t = _ffn(h, norm_ffn[layer], mods, layer, ffn_w_in_b, ffn_w_out_b,
                         kv=(kv_norm, kv_mods, kv_w_b))
        elif layer == depth - 1:
            h = _ffn(h, norm_ffn[layer], mods, layer, ffn_w_in_b, ffn_w_out_b, final_gain=final_norm)
        else:
            h = _ffn(h, norm_ffn[layer], mods, layer, ffn_w_in_b, ffn_w_out_b)
    return h
```

```python
import functools

import numpy as np
import jax
import jax.numpy as jnp
from jax import lax
from jax.experimental import pallas as pl
from jax.experimental.pallas import tpu as pltpu

V7X_LANES = 128
V7X_SUBLANES = 8
V7X_MXU_DIM = 256
V7X_VMEM_LIMIT_BYTES = 56 * 1024 * 1024

CHUNK = 64
A_HEADS = 8
A_KEY_DIM = 128
B_HEADS = 16
B_HEAD_DIM = 64
B_PAST_CHUNKS = 8
REL_CLIP = 256
NORM_EPS = 1e-6
N_MOD = 6
MASK_VALUE = -1e30
MIN_FORGET = 1e-30
LOG2_E = 1.4426950408889634

SUB = V7X_SUBLANES
BLOCKS_PER_CHUNK = CHUNK // SUB
PAIR = 2 * A_KEY_DIM
N_OFF_PAIRS = BLOCKS_PER_CHUNK * (BLOCKS_PER_CHUNK - 1) // 2

HGRN_TILE = 256
ATT_TILE = 256
ATT_WIN = ATT_TILE + B_PAST_CHUNKS * CHUNK
ATT_GROUP = 4
ATT_GROUP_LANES = ATT_GROUP * B_HEAD_DIM
ATT_STAGE_HEADS = 16
ATT_KEY_BLOCK = 128
ATT_ONES_ROWS = 16
ROW_TILE = 512
MOD_MAX_TILE = 2048

f32 = jnp.float32
bf16 = jnp.bfloat16


def _params(*sem):
    return pltpu.CompilerParams(dimension_semantics=sem, vmem_limit_bytes=V7X_VMEM_LIMIT_BYTES)


def _resident(shape, index_map):
    return pl.BlockSpec(shape, index_map, pipeline_mode=pl.Buffered(1))


def _row_spec(d):
    return pl.BlockSpec((1, d), lambda b, i: (0, 0))


def _mod_spec(layer, col, d):
    return pl.BlockSpec((None, None, 1, d), lambda b, i: (layer, b, 0, col))


def _sigmoid(x):
    return 1.0 / (1.0 + jnp.exp(-x))


def _norm_mod(x, gain, shift, scale):
    y = x * lax.rsqrt(jnp.mean(x * x, axis=-1, keepdims=True) + NORM_EPS)
    return (y * gain) * (1.0 + scale) + shift


def _mod_kernel(c_ref, w_ref, b_ref, o_ref):
    c = c_ref[...]
    c_act = c * _sigmoid(c)
    o_ref[...] = jnp.dot(c_act, w_ref[...], preferred_element_type=f32,
                         precision=lax.Precision.HIGHEST) + b_ref[...]


def _modulation(c, w, b):
    n_layers, d, n = w.shape
    bsz = c.shape[0]
    tn = max(t for t in range(V7X_LANES, MOD_MAX_TILE + 1, V7X_LANES) if n % t == 0)
    return pl.pallas_call(
        _mod_kernel,
        grid=(n_layers, n // tn),
        in_specs=[
            pl.BlockSpec((bsz, d), lambda l, j: (0, 0)),
            pl.BlockSpec((None, d, tn), lambda l, j: (l, 0, j)),
            pl.BlockSpec((None, 1, tn), lambda l, j: (l, 0, j)),
        ],
        out_specs=pl.BlockSpec((None, bsz, tn), lambda l, j: (l, 0, j)),
        out_shape=jax.ShapeDtypeStruct((n_layers, bsz, n), f32),
        compiler_params=_params("arbitrary", "arbitrary"),
        name="modulation",
    )(c, w, b.reshape(n_layers, 1, n))


def _ffn_kernel(*refs, ffn_dim, final_norm, with_kv):
    h_ref, gain_ref, shift_ref, scale_ref, gate_ref, win_ref, wout_ref = refs[:7]
    rest = refs[7:]
    x = h_ref[...]
    u = _norm_mod(x, gain_ref[...], shift_ref[...], scale_ref[...]).astype(bf16)
    mid = (ffn_dim // 2) // V7X_MXU_DIM * V7X_MXU_DIM
    acc = None
    for lo, hi in ((0, mid), (mid, ffn_dim)):
        a = jnp.dot(u, win_ref[:, lo:hi], preferred_element_type=f32)
        b = jnp.dot(u, win_ref[:, ffn_dim + lo:ffn_dim + hi], preferred_element_type=f32)
        act = ((a * _sigmoid(a)) * b).astype(bf16)
        p = jnp.dot(act, wout_ref[lo:hi, :], preferred_element_type=f32)
        acc = p if acc is None else acc + p
    out = x + gate_ref[...] * acc
    if with_kv:
        kgain_ref, kshift_ref, kscale_ref, kvw_ref, o_ref, k_ref, vt_ref = rest
        ukv = _norm_mod(out, kgain_ref[...], kshift_ref[...], kscale_ref[...]).astype(bf16)
        kv = jnp.dot(ukv, kvw_ref[...], preferred_element_type=f32)
        d_kv = k_ref.shape[-1]
        k_ref[...] = kv[:, :d_kv].astype(k_ref.dtype)
        vt_ref[...] = kv[:, d_kv:].T.astype(vt_ref.dtype)
    elif final_norm:
        fgain_ref, o_ref = rest
        out = out * lax.rsqrt(jnp.mean(out * out, axis=-1, keepdims=True) + NORM_EPS)
        out = out * fgain_ref[...]
    else:
        (o_ref,) = rest
    o_ref[...] = out


def _ffn(h, gain, mods, layer, w_in, w_out, final_gain=None, kv=None):
    bsz, seq, d = h.shape
    ffn_dim = w_out.shape[1]
    tm = min(ROW_TILE, seq)
    h_spec = pl.BlockSpec((None, tm, d), lambda b, i: (b, i, 0))
    in_specs = [
        h_spec, _row_spec(d), _mod_spec(layer, 3, d), _mod_spec(layer, 4, d), _mod_spec(layer, 5, d),
        _resident((None, d, 2 * ffn_dim), lambda b, i: (layer, 0, 0)),
        _resident((None, ffn_dim, d), lambda b, i: (layer, 0, 0)),
    ]
    args = [h, gain.reshape(1, d), mods, mods, mods, w_in, w_out]
    out_specs = h_spec
    out_shape = jax.ShapeDtypeStruct((bsz, seq, d), f32)
    if kv is not None:
        kv_norm, kv_mods, kv_w = kv
        n_kv = kv_w.shape[-1]
        in_specs += [_row_spec(d), _mod_spec(0, 0, d), _mod_spec(0, 1, d),
                     _resident((d, n_kv), lambda b, i: (0, 0))]
        args += [kv_norm.reshape(1, d), kv_mods, kv_mods, kv_w]
        out_specs = (h_spec, pl.BlockSpec((None, tm, n_kv // 2), lambda b, i: (b, i, 0)),
                     pl.BlockSpec((None, n_kv // 2, tm), lambda b, i: (b, 0, i)))
        out_shape = (out_shape, jax.ShapeDtypeStruct((bsz, seq, n_kv // 2), bf16),
                     jax.ShapeDtypeStruct((bsz, n_kv // 2, seq), bf16))
    elif final_gain is not None:
        in_specs += [_row_spec(d)]
        args += [final_gain.reshape(1, d)]
    return pl.pallas_call(
        functools.partial(_ffn_kernel, ffn_dim=ffn_dim, final_norm=final_gain is not None,
                          with_kv=kv is not None),
        grid=(bsz, seq // tm),
        in_specs=in_specs,
        out_specs=out_specs,
        out_shape=out_shape,
        compiler_params=_params("parallel", "arbitrary"),
        name="ffn",
    )(*args)


def _block_diag2(a, b):
    z = jnp.zeros_like(a)
    return jnp.concatenate([jnp.concatenate([a, z], axis=1),
                            jnp.concatenate([z, b], axis=1)], axis=0)


def _block_cumsum(x3):
    sub = lax.broadcasted_iota(jnp.int32, x3.shape, 1)
    for sh in (1, 2, 4):
        x3 = x3 + jnp.where(sub >= sh, pltpu.roll(x3, sh, axis=1), 0.0)
    return x3


def _hgrn_layer_kernel(h_ref, hnext_ref, gain_ref, shift_ref, scale_ref, gate_ref, win_ref, wout_ref,
                       alb_ref, ogain_ref, wsel_ref, wselw_ref, ones_ref, o_ref,
                       st_ref, y_scr, q_scr, w_scr, c_scr, qt_scr, kt_scr, vb_scr, sg_scr, ctmp_scr,
                       ctot_scr, ginc_scr, lhs_scr, diag_scr, o_scr, *, layer, tile):
    d = A_HEADS * A_KEY_DIM
    nb = tile // SUB
    n_chunks = tile // CHUNK
    n_pairs = A_HEADS // 2

    def in_proj(src_ref):
        u = _norm_mod(src_ref[...], gain_ref[...], shift_ref[...], scale_ref[...]).astype(bf16)
        y_scr[...] = jnp.dot(u, win_ref[...], preferred_element_type=f32)

    @pl.when(pl.program_id(1) == 0)
    def _():
        st_ref[...] = jnp.zeros_like(st_ref)
        in_proj(h_ref)

    a_lb = alb_ref[...]
    e_lb = jnp.exp(a_lb - jnp.max(a_lb, axis=0, keepdims=True))
    sm = e_lb / jnp.sum(e_lb, axis=0, keepdims=True)
    lb = jnp.zeros((1, d), f32)
    for m in range(1, layer + 1):
        lb = lb + sm[m:m + 1, :]
    oml = 1.0 - lb

    qpre = y_scr[:, 0:d]
    zf = y_scr[:, d:2 * d]
    q = qpre * _sigmoid(qpre)
    sig = _sigmoid(zf)
    log_f = jnp.log2(jnp.maximum(lb + oml * sig, MIN_FORGET))
    k_sign = jnp.where(oml < 0.0, -1.0, 1.0)
    log_k = jnp.log2(jnp.abs(oml) * (1.0 - sig))

    c3 = _block_cumsum(log_f.reshape(nb, SUB, d))
    ctot3 = c3[:, SUB - 1:SUB, :]
    w3 = c3 - log_k.reshape(nb, SUB, d)
    q_scr[...] = q * k_sign
    c_scr[...] = c3.reshape(tile, d)
    w_scr[...] = w3.reshape(tile, d)
    qt_scr[...] = (q.reshape(nb, SUB, d) * jnp.exp2(c3)).reshape(tile, d)
    kt_scr[...] = (jnp.exp2(ctot3 - w3).reshape(tile, d)) * k_sign
    vb_scr[...] = y_scr[:, 2 * d:3 * d].astype(bf16)
    g = y_scr[:, 3 * d:4 * d]
    sg_scr[...] = g * _sigmoid(g)
    in_proj(hnext_ref)

    for h in range(d // V7X_LANES):
        ctmp_scr[h] = c_scr[:, h * V7X_LANES:(h + 1) * V7X_LANES]
    ctot = jnp.concatenate([ctmp_scr[h, pl.ds(SUB - 1, nb, stride=SUB), :]
                            for h in range(d // V7X_LANES)], axis=1)
    ctot_scr[...] = ctot
    ginc_scr[...] = _block_cumsum(ctot.reshape(n_chunks, BLOCKS_PER_CHUNK, d)).reshape(nb, d)

    sub = lax.broadcasted_iota(jnp.int32, (nb, SUB, d), 1)
    for s in range(SUB):
        c3 = c_scr[...].reshape(nb, SUB, d)
        w_s = w_scr[...].reshape(nb, SUB, d)[:, s:s + 1, :]
        p = q_scr[...].reshape(nb, SUB, d) * jnp.exp2(jnp.where(sub >= s, c3 - w_s, MASK_VALUE))
        p = p.reshape(tile, d).astype(bf16)
        for m in range(n_pairs):
            lhs_scr[m * tile:(m + 1) * tile, s * PAIR:(s + 1) * PAIR] = p[:, m * PAIR:(m + 1) * PAIR]
    half = n_pairs * tile // 2
    diag_scr[0:half] = jnp.dot(lhs_scr[0:half], wsel_ref[...], preferred_element_type=f32)
    diag_scr[half:] = jnp.dot(lhs_scr[half:], wselw_ref[...],
                              preferred_element_type=f32)[:, :2 * CHUNK]

    lane = lax.broadcasted_iota(jnp.int32, (SUB, V7X_LANES), 1)
    row = lax.broadcasted_iota(jnp.int32, (SUB, V7X_LANES), 0)
    col_block = (lane % CHUNK) // SUB
    col_sub = lane % SUB

    units = [(ci, m) for ci in range(n_chunks) for m in range(n_pairs)]
    per_chunk = []
    for ci in range(n_chunks):
        r0, b0 = ci * CHUNK, ci * BLOCKS_PER_CHUNK
        qt = qt_scr[r0:r0 + CHUNK, :]
        kt = kt_scr[r0:r0 + CHUNK, :]
        ginc = ginc_scr[b0:b0 + BLOCKS_PER_CHUNK, :]
        gexc = ginc - ctot_scr[b0:b0 + BLOCKS_PER_CHUNK, :]
        btot = ginc[BLOCKS_PER_CHUNK - 1:BLOCKS_PER_CHUNK, :]
        e_g = jnp.exp2(gexc)
        e_r = jnp.exp2(btot - ginc)
        qh = jnp.concatenate([qt[SUB * i:SUB * (i + 1)] * e_g[i:i + 1] for i in range(BLOCKS_PER_CHUNK)],
                             axis=0).astype(bf16)
        kh = jnp.concatenate([kt[SUB * i:SUB * (i + 1)] * e_r[i:i + 1] for i in range(BLOCKS_PER_CHUNK)],
                             axis=0).astype(bf16)
        rows = []
        for i in range(1, BLOCKS_PER_CHUNK):
            d_i = jnp.exp2(gexc[i:i + 1] - ginc[0:i])
            for j in range(i):
                rows.append(qt[SUB * i:SUB * (i + 1)] * d_i[j:j + 1])
        lhs_off = jnp.concatenate(rows, axis=0).astype(bf16)
        per_chunk.append(dict(qh=qh, kh=kh, lhs_off=lhs_off, ktb=kt.astype(bf16),
                              vb=vb_scr[r0:r0 + CHUNK, :], e_b=jnp.exp2(btot)))

    def pair_lanes(m):
        return m * PAIR, m * PAIR + A_KEY_DIM, (m + 1) * PAIR

    offs = {}
    for ci, m in units:
        lo, mid, hi = pair_lanes(m)
        c = per_chunk[ci]
        k_bd = _block_diag2(c["ktb"][:, lo:mid], c["ktb"][:, mid:hi])
        offs[ci, m] = lax.dot_general(c["lhs_off"][:, lo:hi], k_bd, (((1,), (1,)), ((), ())),
                                      preferred_element_type=f32)
    a2s = {}
    for ci, m in units:
        off = offs[ci, m]
        dg = diag_scr[m * tile + ci * CHUNK:m * tile + (ci + 1) * CHUNK, :]
        blocks = []
        for i in range(BLOCKS_PER_CHUNK):
            a = jnp.where(col_block == i, jnp.where(col_sub <= row, dg[SUB * i:SUB * (i + 1)], 0.0), 0.0)
            for j in range(i):
                pidx = i * (i - 1) // 2 + j
                a = jnp.where(col_block == j, off[SUB * pidx:SUB * (pidx + 1)], a)
            blocks.append(a)
        a2s[ci, m] = jnp.concatenate(blocks, axis=0).astype(bf16)
    for ci, m in units:
        lo, mid, hi = pair_lanes(m)
        vb = per_chunk[ci]["vb"]
        v_bd = _block_diag2(vb[:, lo:mid], vb[:, mid:hi])
        o_scr[ci * CHUNK:(ci + 1) * CHUNK, lo:hi] = jnp.dot(a2s[ci, m], v_bd,
                                                            preferred_element_type=f32)

    for ci in range(n_chunks):
        c = per_chunk[ci]
        for m in range(n_pairs):
            lo, mid, hi = pair_lanes(m)
            st_e = st_ref[2 * m]
            st_o = st_ref[2 * m + 1]
            st_bd = _block_diag2(st_e.astype(bf16), st_o.astype(bf16))
            o_inter = lax.dot_general(c["qh"][:, lo:hi], st_bd, (((1,), (1,)), ((), ())),
                                      preferred_element_type=f32)
            o_scr[ci * CHUNK:(ci + 1) * CHUNK, lo:hi] += o_inter
            upd = lax.dot_general(c["vb"][:, lo:hi], c["kh"][:, lo:hi], (((0,), (0,)), ((), ())),
                                  preferred_element_type=f32)
            st_ref[2 * m] = c["e_b"][:, lo:mid] * st_e + upd[:A_KEY_DIM, :A_KEY_DIM]
            st_ref[2 * m + 1] = c["e_b"][:, mid:hi] * st_o + upd[A_KEY_DIM:, A_KEY_DIM:]

    o = o_scr[...]
    osq = (o * o).astype(bf16)
    ms = jnp.concatenate(
        [jnp.dot(osq[:, m * PAIR:(m + 1) * PAIR], ones_ref[...], preferred_element_type=f32)
         for m in range(n_pairs)], axis=1) * (1.0 / A_KEY_DIM)
    out = o * lax.rsqrt(ms + NORM_EPS) * ogain_ref[...]
    out = (out * sg_scr[...]).astype(bf16)
    mix = jnp.dot(out, wout_ref[...], preferred_element_type=f32)
    o_ref[...] = h_ref[...] + gate_ref[...] * mix


def _hgrn_selector():
    r = np.arange(SUB * PAIR)
    s_local, h_r = r // PAIR, (r % PAIR) // A_KEY_DIM
    c = np.arange(2 * CHUNK)
    h_c, s_c = c // CHUNK, c % CHUNK
    sel = (h_r[:, None] == h_c[None, :]) & (s_local[:, None] == (s_c % SUB)[None, :])
    return jnp.asarray(sel, dtype=bf16)


def _hgrn_layer(h, gain, mods, layer, w_in, w_out, a_lb, o_gain):
    bsz, seq, d = h.shape
    assert d == A_HEADS * A_KEY_DIM
    tile = min(HGRN_TILE, seq)
    n_pairs = A_HEADS // 2
    n_in = w_in.shape[-1]
    ones_bd = jnp.asarray(np.kron(np.eye(2), np.ones((A_KEY_DIM, A_KEY_DIM))), dtype=bf16)
    gain_t = jnp.tile(o_gain.astype(f32), A_HEADS).reshape(1, d)
    n_a = a_lb.shape[0]
    sel = _hgrn_selector()
    h_spec = pl.BlockSpec((None, tile, d), lambda b, i: (b, i, 0))
    return pl.pallas_call(
        functools.partial(_hgrn_layer_kernel, layer=layer, tile=tile),
        grid=(bsz, seq // tile),
        in_specs=[
            h_spec,
            pl.BlockSpec((None, tile, d), lambda b, i: (b, jnp.minimum(i + 1, seq // tile - 1), 0)),
            _row_spec(d), _mod_spec(layer, 0, d), _mod_spec(layer, 1, d),
            _mod_spec(layer, 2, d),
            _resident((None, d, n_in), lambda b, i: (layer, 0, 0)),
            _resident((None, d, d), lambda b, i: (layer, 0, 0)),
            pl.BlockSpec((n_a, d), lambda b, i: (0, 0)),
            _row_spec(d),
            _resident((SUB * PAIR, 2 * CHUNK), lambda b, i: (0, 0)),
            _resident((SUB * PAIR, 4 * CHUNK), lambda b, i: (0, 0)),
            _resident((PAIR, PAIR), lambda b, i: (0, 0)),
        ],
        out_specs=h_spec,
        out_shape=jax.ShapeDtypeStruct((bsz, seq, d), f32),
        scratch_shapes=[
            pltpu.VMEM((A_HEADS, A_KEY_DIM, A_KEY_DIM), f32),
            pltpu.VMEM((tile, n_in), f32),
            pltpu.VMEM((tile, d), f32),
            pltpu.VMEM((tile, d), f32),
            pltpu.VMEM((tile, d), f32),
            pltpu.VMEM((tile, d), f32),
            pltpu.VMEM((tile, d), f32),
            pltpu.VMEM((tile, d), bf16),
            pltpu.VMEM((tile, d), f32),
            pltpu.VMEM((d // V7X_LANES, tile, V7X_LANES), f32),
            pltpu.VMEM((tile // SUB, d), f32),
            pltpu.VMEM((tile // SUB, d), f32),
            pltpu.VMEM((n_pairs * tile, SUB * PAIR), bf16),
            pltpu.VMEM((n_pairs * tile, 2 * CHUNK), f32),
            pltpu.VMEM((tile, d), f32),
        ],
        compiler_params=_params("parallel", "arbitrary"),
        name="hgrn_layer",
    )(h, h, gain.reshape(1, d), mods, mods, mods, w_in, w_out, a_lb.astype(f32), gain_t,
      sel, jnp.concatenate([sel, jnp.zeros_like(sel)], axis=1), ones_bd)


def _attn_layer_kernel(h_ref, hnext_ref, gain_ref, shift_ref, scale_ref, gate_ref, wq_ref, wo_ref,
                       k0_ref, k1_ref, k2_ref, vt0_ref, vt1_ref, vt2_ref, bias_ref, o_ref,
                       q_scr, qnext_scr, biasp_scr, ot_scr):
    i = pl.program_id(1)

    def q_proj(src_ref):
        u = _norm_mod(src_ref[...], gain_ref[...], shift_ref[...], scale_ref[...]).astype(bf16)
        q = jnp.dot(u, wq_ref[...], preferred_element_type=f32) * (B_HEAD_DIM ** -0.5 * LOG2_E)
        qnext_scr[...] = q.astype(bf16)

    @pl.when(i == 0)
    def _():
        q_proj(h_ref)

    @pl.when(i <= ATT_WIN // ATT_TILE - 1)
    def _():
        krow = lax.broadcasted_iota(jnp.int32, (ATT_WIN, ATT_TILE), 0)
        pad = jnp.where(krow >= (ATT_WIN // ATT_TILE - 1 - i) * ATT_TILE, 0.0, MASK_VALUE)
        for hd in range(B_HEADS):
            biasp_scr[hd] = bias_ref[hd] + pad

    q_scr[...] = qnext_scr[...]
    q_proj(hnext_ref)

    lane = lax.broadcasted_iota(jnp.int32, (1, ATT_GROUP_LANES), 1)
    head_of_lane = lane // B_HEAD_DIM
    ones_rows = jnp.ones((ATT_ONES_ROWS, ATT_WIN), bf16)
    for h0 in range(0, B_HEADS, ATT_STAGE_HEADS):
        heads = list(range(h0, h0 + ATT_STAGE_HEADS))
        s_ts = []
        for head in heads:
            g, hh = divmod(head, ATT_GROUP)
            lanes = slice(g * ATT_GROUP_LANES, (g + 1) * ATT_GROUP_LANES)
            kwin = jnp.concatenate([k0_ref[:, lanes], k1_ref[:, lanes], k2_ref[:, lanes]], axis=0)
            keep = jnp.where(head_of_lane == hh, 1.0, 0.0).astype(bf16)
            s_ts.append(lax.dot_general(kwin, q_scr[:, lanes] * keep, (((1,), (1,)), ((), ())),
                                        preferred_element_type=f32))
        partials = []
        for head, s_t in zip(heads, s_ts):
            rows = slice(head * B_HEAD_DIM, (head + 1) * B_HEAD_DIM)
            v_t = jnp.concatenate([vt0_ref[rows, :], vt1_ref[rows, :], vt2_ref[rows, :]], axis=1)
            v_ext = jnp.concatenate([v_t, ones_rows], axis=0)
            blocks = []
            for k0 in range(0, ATT_WIN, ATT_KEY_BLOCK):
                keys = slice(k0, k0 + ATT_KEY_BLOCK)
                s_b = s_t[keys, :] + biasp_scr[head, keys, :]
                m_b = jnp.max(s_b, axis=0, keepdims=True)
                p_b = jnp.exp2(s_b - m_b).astype(bf16)
                blocks.append((m_b, jnp.dot(v_ext[:, keys], p_b, preferred_element_type=f32)))
            partials.append(blocks)
        for head, blocks in zip(heads, partials):
            rows = slice(head * B_HEAD_DIM, (head + 1) * B_HEAD_DIM)
            m = blocks[0][0]
            for m_b, _ in blocks[1:]:
                m = jnp.maximum(m, m_b)
            o_t = None
            for m_b, o_b in blocks:
                term = o_b * jnp.exp2(m_b - m)
                o_t = term if o_t is None else o_t + term
            ot_scr[rows, :] = o_t[:B_HEAD_DIM] * (1.0 / o_t[B_HEAD_DIM:B_HEAD_DIM + 1])
    mix = lax.dot_general(ot_scr[...].astype(bf16), wo_ref[...], (((0,), (0,)), ((), ())),
                          preferred_element_type=f32)
    o_ref[...] = h_ref[...] + gate_ref[...] * mix


def _attn_bias_table(rel_bias):
    n_rel, n_heads = rel_bias.shape
    span = ATT_TILE + ATT_WIN
    j = np.arange(span)
    delta = np.where(j < ATT_WIN, j, j - span)
    idx = np.clip(delta - B_PAST_CHUNKS * CHUNK, -REL_CLIP, CHUNK - 1) + REL_CLIP
    e = (rel_bias.astype(f32) * LOG2_E)[jnp.asarray(idx)].T
    flat = jnp.tile(e, (1, ATT_TILE))[:, :ATT_TILE * (span - 1)]
    table = flat.reshape(n_heads, ATT_TILE, span - 1)[:, :, :ATT_WIN]
    qchunk = np.arange(ATT_TILE)[:, None] // CHUNK + B_PAST_CHUNKS
    back = qchunk - np.arange(ATT_WIN)[None, :] // CHUNK
    in_band = (back >= 0) & (back <= B_PAST_CHUNKS)
    return jnp.swapaxes(jnp.where(jnp.asarray(in_band)[None], table, MASK_VALUE), 1, 2)


def _attn_layer(h, gain, mods, layer, w_q, w_o, w_layer, k, v_t, rel_bias):
    bsz, seq, d = h.shape
    assert seq % ATT_TILE == 0 and d == B_HEADS * B_HEAD_DIM
    bias = _attn_bias_table(rel_bias)
    h_spec = pl.BlockSpec((None, ATT_TILE, d), lambda b, i: (b, i, 0))

    def k_spec(back):
        return pl.BlockSpec((None, ATT_TILE, d), lambda b, i: (b, jnp.maximum(i - back, 0), 0))

    def vt_spec(back):
        return pl.BlockSpec((None, d, ATT_TILE), lambda b, i: (b, 0, jnp.maximum(i - back, 0)))

    return pl.pallas_call(
        _attn_layer_kernel,
        grid=(bsz, seq // ATT_TILE),
        in_specs=[
            h_spec,
            pl.BlockSpec((None, ATT_TILE, d), lambda b, i: (b, jnp.minimum(i + 1, seq // ATT_TILE - 1), 0)),
            _row_spec(d), _mod_spec(layer, 0, d), _mod_spec(layer, 1, d),
            _mod_spec(layer, 2, d),
            _resident((None, d, d), lambda b, i: (w_layer, 0, 0)),
            _resident((None, d, d), lambda b, i: (w_layer, 0, 0)),
            k_spec(2), k_spec(1), k_spec(0),
            vt_spec(2), vt_spec(1), vt_spec(0),
            _resident((B_HEADS, ATT_WIN, ATT_TILE), lambda b, i: (0, 0, 0)),
        ],
        out_specs=h_spec,
        out_shape=jax.ShapeDtypeStruct((bsz, seq, d), f32),
        scratch_shapes=[
            pltpu.VMEM((ATT_TILE, d), bf16),
            pltpu.VMEM((ATT_TILE, d), bf16),
            pltpu.VMEM((B_HEADS, ATT_WIN, ATT_TILE), f32),
            pltpu.VMEM((d, ATT_TILE), f32),
        ],
        compiler_params=_params("parallel", "arbitrary"),
        name="attn_layer",
    )(h, h, gain.reshape(1, d), mods, mods, mods, w_q, w_o, k, k, k, v_t, v_t, v_t, bias)


def kernel(x, c, mod_w, mod_b, norm_mix, norm_ffn, ffn_w_in, ffn_w_out, a_w_in, a_w_out, a_lb,
           a_out_norm, kv_norm, kv_mod_w, kv_mod_b, kv_w, b_w_q, b_w_o, b_rel_bias, final_norm):
    depth = mod_w.shape[0]
    n_a = a_w_in.shape[0]
    bsz = x.shape[0]

    mods = _modulation(c, mod_w, mod_b).reshape(depth, bsz, 1, -1)
    kv_mods = _modulation(c, kv_mod_w[None], kv_mod_b[None]).reshape(1, bsz, 1, -1)

    ffn_w_in_b = ffn_w_in.astype(bf16)
    ffn_w_out_b = ffn_w_out.astype(bf16)
    a_w_in_b = a_w_in.astype(bf16)
    a_w_out_b = a_w_out.astype(bf16)
    kv_w_b = kv_w.astype(bf16)
    b_w_q_b = b_w_q.astype(bf16)
    b_w_o_b = b_w_o.astype(bf16)

    h = x
    k = v_t = None
    for layer in range(depth):
        if layer < n_a:
            h = _hgrn_layer(h, norm_mix[layer], mods, layer, a_w_in_b, a_w_out_b, a_lb,
                            a_out_norm[layer])
        else:
            j = layer - n_a
            h = _attn_layer(h, norm_mix[layer], mods, layer, b_w_q_b, b_w_o_b, j, k, v_t,
                            b_rel_bias[j])
        if layer == n_a - 1:
            h, k, v_t = _ffn(h, norm_ffn[layer], mods, layer, ffn_w_in_b, ffn_w_out_b,
                         kv=(kv_norm, kv_mods, kv_w_b))
        elif layer == depth - 1:
            h = _ffn(h, norm_ffn[layer], mods, layer, ffn_w_in_b, ffn_w_out_b, final_gain=final_norm)
        else:
            h = _ffn(h, norm_ffn[layer], mods, layer, ffn_w_in_b, ffn_w_out_b)
    return h
```

```python
import functools

import numpy as np
import jax
import jax.numpy as jnp
from jax import lax
from jax.experimental import pallas as pl
from jax.experimental.pallas import tpu as pltpu

V7X_LANES = 128
V7X_SUBLANES = 8
V7X_MXU_DIM = 256
V7X_VMEM_LIMIT_BYTES = 56 * 1024 * 1024

CHUNK = 64
A_HEADS = 8
A_KEY_DIM = 128
B_HEADS = 16
B_HEAD_DIM = 64
B_PAST_CHUNKS = 8
REL_CLIP = 256
NORM_EPS = 1e-6
N_MOD = 6
MASK_VALUE = -1e30
MIN_FORGET = 1e-30
LOG2_E = 1.4426950408889634

SUB = V7X_SUBLANES
BLOCKS_PER_CHUNK = CHUNK // SUB
PAIR = 2 * A_KEY_DIM
N_OFF_PAIRS = BLOCKS_PER_CHUNK * (BLOCKS_PER_CHUNK - 1) // 2

HGRN_TILE = 256
ATT_TILE = 256
ATT_WIN = ATT_TILE + B_PAST_CHUNKS * CHUNK
ATT_GROUP = 4
ATT_GROUP_LANES = ATT_GROUP * B_HEAD_DIM
ATT_STAGE_HEADS = 16
ATT_KEY_BLOCK = 128
ATT_ONES_ROWS = 16
ROW_TILE = 512
MOD_MAX_TILE = 2048

f32 = jnp.float32
bf16 = jnp.bfloat16


def _params(*sem):
    return pltpu.CompilerParams(dimension_semantics=sem, vmem_limit_bytes=V7X_VMEM_LIMIT_BYTES)


def _resident(shape, index_map):
    return pl.BlockSpec(shape, index_map, pipeline_mode=pl.Buffered(1))


def _row_spec(d):
    return pl.BlockSpec((1, d), lambda b, i: (0, 0))


def _mod_spec(layer, col, d):
    return pl.BlockSpec((None, None, 1, d), lambda b, i: (layer, b, 0, col))


def _sigmoid(x):
    return 1.0 / (1.0 + jnp.exp(-x))


def _norm_mod(x, gain, shift, scale):
    y = x * lax.rsqrt(jnp.mean(x * x, axis=-1, keepdims=True) + NORM_EPS)
    return (y * gain) * (1.0 + scale) + shift


def _mod_kernel(c_ref, w_ref, b_ref, o_ref):
    c = c_ref[...]
    c_act = c * _sigmoid(c)
    o_ref[...] = jnp.dot(c_act, w_ref[...], preferred_element_type=f32,
                         precision=lax.Precision.HIGHEST) + b_ref[...]


def _modulation(c, w, b):
    n_layers, d, n = w.shape
    bsz = c.shape[0]
    tn = max(t for t in range(V7X_LANES, MOD_MAX_TILE + 1, V7X_LANES) if n % t == 0)
    return pl.pallas_call(
        _mod_kernel,
        grid=(n_layers, n // tn),
        in_specs=[
            pl.BlockSpec((bsz, d), lambda l, j: (0, 0)),
            pl.BlockSpec((None, d, tn), lambda l, j: (l, 0, j)),
            pl.BlockSpec((None, 1, tn), lambda l, j: (l, 0, j)),
        ],
        out_specs=pl.BlockSpec((None, bsz, tn), lambda l, j: (l, 0, j)),
        out_shape=jax.ShapeDtypeStruct((n_layers, bsz, n), f32),
        compiler_params=_params("arbitrary", "arbitrary"),
        name="modulation",
    )(c, w, b.reshape(n_layers, 1, n))


def _ffn_kernel(*refs, ffn_dim, final_norm, with_kv):
    h_ref, gain_ref, shift_ref, scale_ref, gate_ref, win_ref, wout_ref = refs[:7]
    rest = refs[7:]
    x = h_ref[...]
    u = _norm_mod(x, gain_ref[...], shift_ref[...], scale_ref[...]).astype(bf16)
    mid = (ffn_dim // 2) // V7X_MXU_DIM * V7X_MXU_DIM
    acc = None
    for lo, hi in ((0, mid), (mid, ffn_dim)):
        a = jnp.dot(u, win_ref[:, lo:hi], preferred_element_type=f32)
        b = jnp.dot(u, win_ref[:, ffn_dim + lo:ffn_dim + hi], preferred_element_type=f32)
        act = ((a * _sigmoid(a)) * b).astype(bf16)
        p = jnp.dot(act, wout_ref[lo:hi, :], preferred_element_type=f32)
        acc = p if acc is None else acc + p
    out = x + gate_ref[...] * acc
    if with_kv:
        kgain_ref, kshift_ref, kscale_ref, kvw_ref, o_ref, k_ref, vt_ref = rest
        ukv = _norm_mod(out, kgain_ref[...], kshift_ref[...], kscale_ref[...]).astype(bf16)
        kv = jnp.dot(ukv, kvw_ref[...], preferred_element_type=f32)
        d_kv = k_ref.shape[-1]
        k_ref[...] = kv[:, :d_kv].astype(k_ref.dtype)
        vt_ref[...] = kv[:, d_kv:].T.astype(vt_ref.dtype)
    elif final_norm:
        fgain_ref, o_ref = rest
        out = out * lax.rsqrt(jnp.mean(out * out, axis=-1, keepdims=True) + NORM_EPS)
        out = out * fgain_ref[...]
    else:
        (o_ref,) = rest
    o_ref[...] = out


def _ffn(h, gain, mods, layer, w_in, w_out, final_gain=None, kv=None):
    bsz, seq, d = h.shape
    ffn_dim = w_out.shape[1]
    tm = min(ROW_TILE, seq)
    h_spec = pl.BlockSpec((None, tm, d), lambda b, i: (b, i, 0))
    in_specs = [
        h_spec, _row_spec(d), _mod_spec(layer, 3, d), _mod_spec(layer, 4, d), _mod_spec(layer, 5, d),
        _resident((None, d, 2 * ffn_dim), lambda b, i: (layer, 0, 0)),
        _resident((None, ffn_dim, d), lambda b, i: (layer, 0, 0)),
    ]
    args = [h, gain.reshape(1, d), mods, mods, mods, w_in, w_out]
    out_specs = h_spec
    out_shape = jax.ShapeDtypeStruct((bsz, seq, d), f32)
    if kv is not None:
        kv_norm, kv_mods, kv_w = kv
        n_kv = kv_w.shape[-1]
        in_specs += [_row_spec(d), _mod_spec(0, 0, d), _mod_spec(0, 1, d),
                     _resident((d, n_kv), lambda b, i: (0, 0))]
        args += [kv_norm.reshape(1, d), kv_mods, kv_mods, kv_w]
        out_specs = (h_spec, pl.BlockSpec((None, tm, n_kv // 2), lambda b, i: (b, i, 0)),
                     pl.BlockSpec((None, n_kv // 2, tm), lambda b, i: (b, 0, i)))
        out_shape = (out_shape, jax.ShapeDtypeStruct((bsz, seq, n_kv // 2), bf16),
                     jax.ShapeDtypeStruct((bsz, n_kv // 2, seq), bf16))
    elif final_gain is not None:
        in_specs += [_row_spec(d)]
        args += [final_gain.reshape(1, d)]
    return pl.pallas_call(
        functools.partial(_ffn_kernel, ffn_dim=ffn_dim, final_norm=final_gain is not None,
                          with_kv=kv is not None),
        grid=(bsz, seq // tm),
        in_specs=in_specs,
        out_specs=out_specs,
        out_shape=out_shape,
        compiler_params=_params("parallel", "arbitrary"),
        name="ffn",
    )(*args)


def _block_diag2(a, b):
    z = jnp.zeros_like(a)
    return jnp.concatenate([jnp.concatenate([a, z], axis=1),
                            jnp.concatenate([z, b], axis=1)], axis=0)


def _block_cumsum(x3):
    sub = lax.broadcasted_iota(jnp.int32, x3.shape, 1)
    for sh in (1, 2, 4):
        x3 = x3 + jnp.where(sub >= sh, pltpu.roll(x3, sh, axis=1), 0.0)
    return x3


def _hgrn_layer_kernel(h_ref, hnext_ref, gain_ref, shift_ref, scale_ref, gate_ref, win_ref, wout_ref,
                       alb_ref, ogain_ref, wsel_ref, wselw_ref, ones_ref, o_ref,
                       st_ref, y_scr, q_scr, w_scr, c_scr, qt_scr, kt_scr, vb_scr, sg_scr, ctmp_scr,
                       ctot_scr, ginc_scr, lhs_scr, diag_scr, o_scr, *, layer, tile):
    d = A_HEADS * A_KEY_DIM
    nb = tile // SUB
    n_chunks = tile // CHUNK
    n_pairs = A_HEADS // 2

    def in_proj(src_ref):
        u = _norm_mod(src_ref[...], gain_ref[...], shift_ref[...], scale_ref[...]).astype(bf16)
        y_scr[...] = jnp.dot(u, win_ref[...], preferred_element_type=f32)

    @pl.when(pl.program_id(1) == 0)
    def _():
        st_ref[...] = jnp.zeros_like(st_ref)
        in_proj(h_ref)

    a_lb = alb_ref[...]
    e_lb = jnp.exp(a_lb - jnp.max(a_lb, axis=0, keepdims=True))
    sm = e_lb / jnp.sum(e_lb, axis=0, keepdims=True)
    lb = jnp.zeros((1, d), f32)
    for m in range(1, layer + 1):
        lb = lb + sm[m:m + 1, :]
    oml = 1.0 - lb

    qpre = y_scr[:, 0:d]
    zf = y_scr[:, d:2 * d]
    q = qpre * _sigmoid(qpre)
    sig = _sigmoid(zf)
    log_f = jnp.log2(jnp.maximum(lb + oml * sig, MIN_FORGET))
    k_sign = jnp.where(oml < 0.0, -1.0, 1.0)
    log_k = jnp.log2(jnp.abs(oml) * (1.0 - sig))

    c3 = _block_cumsum(log_f.reshape(nb, SUB, d))
    ctot3 = c3[:, SUB - 1:SUB, :]
    w3 = c3 - log_k.reshape(nb, SUB, d)
    q_scr[...] = (q * k_sign).astype(bf16)
    c_scr[...] = c3.reshape(tile, d)
    w_scr[...] = w3.reshape(tile, d)
    qt_scr[...] = (q.reshape(nb, SUB, d) * jnp.exp2(c3)).reshape(tile, d)
    kt_scr[...] = (jnp.exp2(ctot3 - w3).reshape(tile, d)) * k_sign
    vb_scr[...] = y_scr[:, 2 * d:3 * d].astype(bf16)
    g = y_scr[:, 3 * d:4 * d]
    sg_scr[...] = g * _sigmoid(g)
    in_proj(hnext_ref)

    for h in range(d // V7X_LANES):
        ctmp_scr[h] = c_scr[:, h * V7X_LANES:(h + 1) * V7X_LANES]
    ctot = jnp.concatenate([ctmp_scr[h, pl.ds(SUB - 1, nb, stride=SUB), :]
                            for h in range(d // V7X_LANES)], axis=1)
    ctot_scr[...] = ctot
    ginc_scr[...] = _block_cumsum(ctot.reshape(n_chunks, BLOCKS_PER_CHUNK, d)).reshape(nb, d)

    sub = lax.broadcasted_iota(jnp.int32, (nb, SUB, d), 1)
    for s in range(SUB):
        c3 = c_scr[...].reshape(nb, SUB, d)
        w_s = w_scr[...].reshape(nb, SUB, d)[:, s:s + 1, :]
        decay = jnp.exp2(jnp.where(sub >= s, c3 - w_s, MASK_VALUE))
        p = decay.reshape(tile, d).astype(bf16) * q_scr[...]
        for m in range(n_pairs):
            lhs_scr[m * tile:(m + 1) * tile, s * PAIR:(s + 1) * PAIR] = p[:, m * PAIR:(m + 1) * PAIR]
    half = n_pairs * tile // 2
    diag_scr[0:half] = jnp.dot(lhs_scr[0:half], wsel_ref[...], preferred_element_type=f32)
    diag_scr[half:] = jnp.dot(lhs_scr[half:], wselw_ref[...],
                              preferred_element_type=f32)[:, :2 * CHUNK]

    lane = lax.broadcasted_iota(jnp.int32, (SUB, V7X_LANES), 1)
    row = lax.broadcasted_iota(jnp.int32, (SUB, V7X_LANES), 0)
    col_block = (lane % CHUNK) // SUB
    col_sub = lane % SUB

    units = [(ci, m) for ci in range(n_chunks) for m in range(n_pairs)]
    per_chunk = []
    for ci in range(n_chunks):
        r0, b0 = ci * CHUNK, ci * BLOCKS_PER_CHUNK
        qt = qt_scr[r0:r0 + CHUNK, :]
        kt = kt_scr[r0:r0 + CHUNK, :]
        ginc = ginc_scr[b0:b0 + BLOCKS_PER_CHUNK, :]
        gexc = ginc - ctot_scr[b0:b0 + BLOCKS_PER_CHUNK, :]
        btot = ginc[BLOCKS_PER_CHUNK - 1:BLOCKS_PER_CHUNK, :]
        e_g = jnp.exp2(gexc)
        e_r = jnp.exp2(btot - ginc)
        qh = jnp.concatenate([qt[SUB * i:SUB * (i + 1)] * e_g[i:i + 1] for i in range(BLOCKS_PER_CHUNK)],
                             axis=0).astype(bf16)
        kh = jnp.concatenate([kt[SUB * i:SUB * (i + 1)] * e_r[i:i + 1] for i in range(BLOCKS_PER_CHUNK)],
                             axis=0).astype(bf16)
        rows = []
        for i in range(1, BLOCKS_PER_CHUNK):
            d_i = jnp.exp2(gexc[i:i + 1] - ginc[0:i])
            for j in range(i):
                rows.append(qt[SUB * i:SUB * (i + 1)] * d_i[j:j + 1])
        lhs_off = jnp.concatenate(rows, axis=0).astype(bf16)
        per_chunk.append(dict(qh=qh, kh=kh, lhs_off=lhs_off, ktb=kt.astype(bf16),
                              vb=vb_scr[r0:r0 + CHUNK, :], e_b=jnp.exp2(btot)))

    def pair_lanes(m):
        return m * PAIR, m * PAIR + A_KEY_DIM, (m + 1) * PAIR

    offs = {}
    for ci, m in units:
        lo, mid, hi = pair_lanes(m)
        c = per_chunk[ci]
        k_bd = _block_diag2(c["ktb"][:, lo:mid], c["ktb"][:, mid:hi])
        offs[ci, m] = lax.dot_general(c["lhs_off"][:, lo:hi], k_bd, (((1,), (1,)), ((), ())),
                                      preferred_element_type=f32)
    a2s = {}
    for ci, m in units:
        off = offs[ci, m]
        dg = diag_scr[m * tile + ci * CHUNK:m * tile + (ci + 1) * CHUNK, :]
        blocks = []
        for i in range(BLOCKS_PER_CHUNK):
            a = jnp.where(col_block == i, jnp.where(col_sub <= row, dg[SUB * i:SUB * (i + 1)], 0.0), 0.0)
            for j in range(i):
                pidx = i * (i - 1) // 2 + j
                a = jnp.where(col_block == j, off[SUB * pidx:SUB * (pidx + 1)], a)
            blocks.append(a)
        a2s[ci, m] = jnp.concatenate(blocks, axis=0).astype(bf16)
    for ci, m in units:
        lo, mid, hi = pair_lanes(m)
        vb = per_chunk[ci]["vb"]
        v_bd = _block_diag2(vb[:, lo:mid], vb[:, mid:hi])
        o_scr[ci * CHUNK:(ci + 1) * CHUNK, lo:hi] = jnp.dot(a2s[ci, m], v_bd,
                                                            preferred_element_type=f32)

    for ci in range(n_chunks):
        c = per_chunk[ci]
        for m in range(n_pairs):
            lo, mid, hi = pair_lanes(m)
            st_e = st_ref[2 * m]
            st_o = st_ref[2 * m + 1]
            st_bd = _block_diag2(st_e.astype(bf16), st_o.astype(bf16))
            o_inter = lax.dot_general(c["qh"][:, lo:hi], st_bd, (((1,), (1,)), ((), ())),
                                      preferred_element_type=f32)
            o_scr[ci * CHUNK:(ci + 1) * CHUNK, lo:hi] += o_inter
            upd = lax.dot_general(c["vb"][:, lo:hi], c["kh"][:, lo:hi], (((0,), (0,)), ((), ())),
                                  preferred_element_type=f32)
            st_ref[2 * m] = c["e_b"][:, lo:mid] * st_e + upd[:A_KEY_DIM, :A_KEY_DIM]
            st_ref[2 * m + 1] = c["e_b"][:, mid:hi] * st_o + upd[A_KEY_DIM:, A_KEY_DIM:]

    o = o_scr[...]
    osq = (o * o).astype(bf16)
    ms = jnp.concatenate(
        [jnp.dot(osq[:, m * PAIR:(m + 1) * PAIR], ones_ref[...], preferred_element_type=f32)
         for m in range(n_pairs)], axis=1) * (1.0 / A_KEY_DIM)
    out = o * lax.rsqrt(ms + NORM_EPS) * ogain_ref[...]
    out = (out * sg_scr[...]).astype(bf16)
    mix = jnp.dot(out, wout_ref[...], preferred_element_type=f32)
    o_ref[...] = h_ref[...] + gate_ref[...] * mix


def _hgrn_selector():
    r = np.arange(SUB * PAIR)
    s_local, h_r = r // PAIR, (r % PAIR) // A_KEY_DIM
    c = np.arange(2 * CHUNK)
    h_c, s_c = c // CHUNK, c % CHUNK
    sel = (h_r[:, None] == h_c[None, :]) & (s_local[:, None] == (s_c % SUB)[None, :])
    return jnp.asarray(sel, dtype=bf16)


def _hgrn_layer(h, gain, mods, layer, w_in, w_out, a_lb, o_gain):
    bsz, seq, d = h.shape
    assert d == A_HEADS * A_KEY_DIM
    tile = min(HGRN_TILE, seq)
    n_pairs = A_HEADS // 2
    n_in = w_in.shape[-1]
    ones_bd = jnp.asarray(np.kron(np.eye(2), np.ones((A_KEY_DIM, A_KEY_DIM))), dtype=bf16)
    gain_t = jnp.tile(o_gain.astype(f32), A_HEADS).reshape(1, d)
    n_a = a_lb.shape[0]
    sel = _hgrn_selector()
    h_spec = pl.BlockSpec((None, tile, d), lambda b, i: (b, i, 0))
    return pl.pallas_call(
        functools.partial(_hgrn_layer_kernel, layer=layer, tile=tile),
        grid=(bsz, seq // tile),
        in_specs=[
            h_spec,
            pl.BlockSpec((None, tile, d), lambda b, i: (b, jnp.minimum(i + 1, seq // tile - 1), 0)),
            _row_spec(d), _mod_spec(layer, 0, d), _mod_spec(layer, 1, d),
            _mod_spec(layer, 2, d),
            _resident((None, d, n_in), lambda b, i: (layer, 0, 0)),
            _resident((None, d, d), lambda b, i: (layer, 0, 0)),
            pl.BlockSpec((n_a, d), lambda b, i: (0, 0)),
            _row_spec(d),
            _resident((SUB * PAIR, 2 * CHUNK), lambda b, i: (0, 0)),
            _resident((SUB * PAIR, 4 * CHUNK), lambda b, i: (0, 0)),
            _resident((PAIR, PAIR), lambda b, i: (0, 0)),
        ],
        out_specs=h_spec,
        out_shape=jax.ShapeDtypeStruct((bsz, seq, d), f32),
        scratch_shapes=[
            pltpu.VMEM((A_HEADS, A_KEY_DIM, A_KEY_DIM), f32),
            pltpu.VMEM((tile, n_in), f32),
            pltpu.VMEM((tile, d), bf16),
            pltpu.VMEM((tile, d), f32),
            pltpu.VMEM((tile, d), f32),
            pltpu.VMEM((tile, d), f32),
            pltpu.VMEM((tile, d), f32),
            pltpu.VMEM((tile, d), bf16),
            pltpu.VMEM((tile, d), f32),
            pltpu.VMEM((d // V7X_LANES, tile, V7X_LANES), f32),
            pltpu.VMEM((tile // SUB, d), f32),
            pltpu.VMEM((tile // SUB, d), f32),
            pltpu.VMEM((n_pairs * tile, SUB * PAIR), bf16),
            pltpu.VMEM((n_pairs * tile, 2 * CHUNK), f32),
            pltpu.VMEM((tile, d), f32),
        ],
        compiler_params=_params("parallel", "arbitrary"),
        name="hgrn_layer",
    )(h, h, gain.reshape(1, d), mods, mods, mods, w_in, w_out, a_lb.astype(f32), gain_t,
      sel, jnp.concatenate([sel, jnp.zeros_like(sel)], axis=1), ones_bd)


def _attn_layer_kernel(h_ref, hnext_ref, gain_ref, shift_ref, scale_ref, gate_ref, wq_ref, wo_ref,
                       k0_ref, k1_ref, k2_ref, vt0_ref, vt1_ref, vt2_ref, bias_ref, o_ref,
                       q_scr, qnext_scr, biasp_scr, ot_scr):
    i = pl.program_id(1)

    def q_proj(src_ref):
        u = _norm_mod(src_ref[...], gain_ref[...], shift_ref[...], scale_ref[...]).astype(bf16)
        q = jnp.dot(u, wq_ref[...], preferred_element_type=f32) * (B_HEAD_DIM ** -0.5 * LOG2_E)
        qnext_scr[...] = q.astype(bf16)

    @pl.when(i == 0)
    def _():
        q_proj(h_ref)

    @pl.when(i <= ATT_WIN // ATT_TILE - 1)
    def _():
        krow = lax.broadcasted_iota(jnp.int32, (ATT_WIN, ATT_TILE), 0)
        pad = jnp.where(krow >= (ATT_WIN // ATT_TILE - 1 - i) * ATT_TILE, 0.0, MASK_VALUE)
        for hd in range(B_HEADS):
            biasp_scr[hd] = bias_ref[hd] + pad

    q_scr[...] = qnext_scr[...]
    q_proj(hnext_ref)

    lane = lax.broadcasted_iota(jnp.int32, (1, ATT_GROUP_LANES), 1)
    head_of_lane = lane // B_HEAD_DIM
    ones_rows = jnp.ones((ATT_ONES_ROWS, ATT_WIN), bf16)
    for h0 in range(0, B_HEADS, ATT_STAGE_HEADS):
        heads = list(range(h0, h0 + ATT_STAGE_HEADS))
        s_ts = []
        for head in heads:
            g, hh = divmod(head, ATT_GROUP)
            lanes = slice(g * ATT_GROUP_LANES, (g + 1) * ATT_GROUP_LANES)
            kwin = jnp.concatenate([k0_ref[:, lanes], k1_ref[:, lanes], k2_ref[:, lanes]], axis=0)
            keep = jnp.where(head_of_lane == hh, 1.0, 0.0).astype(bf16)
            s_ts.append(lax.dot_general(kwin, q_scr[:, lanes] * keep, (((1,), (1,)), ((), ())),
                                        preferred_element_type=f32))
        partials = []
        for head, s_t in zip(heads, s_ts):
            rows = slice(head * B_HEAD_DIM, (head + 1) * B_HEAD_DIM)
            v_t = jnp.concatenate([vt0_ref[rows, :], vt1_ref[rows, :], vt2_ref[rows, :]], axis=1)
            v_ext = jnp.concatenate([v_t, ones_rows], axis=0)
            blocks = []
            for k0 in range(0, ATT_WIN, ATT_KEY_BLOCK):
                keys = slice(k0, k0 + ATT_KEY_BLOCK)
                s_b = s_t[keys, :] + biasp_scr[head, keys, :]
                m_b = jnp.max(s_b, axis=0, keepdims=True)
                p_b = jnp.exp2(s_b - m_b).astype(bf16)
                blocks.append((m_b, jnp.dot(v_ext[:, keys], p_b, preferred_element_type=f32)))
            partials.append(blocks)
        for head, blocks in zip(heads, partials):
            rows = slice(head * B_HEAD_DIM, (head + 1) * B_HEAD_DIM)
            m = blocks[0][0]
            for m_b, _ in blocks[1:]:
                m = jnp.maximum(m, m_b)
            o_t = None
            for m_b, o_b in blocks:
                term = o_b * jnp.exp2(m_b - m)
                o_t = term if o_t is None else o_t + term
            ot_scr[rows, :] = o_t[:B_HEAD_DIM] * (1.0 / o_t[B_HEAD_DIM:B_HEAD_DIM + 1])
    mix = lax.dot_general(ot_scr[...].astype(bf16), wo_ref[...], (((0,), (0,)), ((), ())),
                          preferred_element_type=f32)
    o_ref[...] = h_ref[...] + gate_ref[...] * mix


def _attn_bias_table(rel_bias):
    n_rel, n_heads = rel_bias.shape
    span = ATT_TILE + ATT_WIN
    j = np.arange(span)
    delta = np.where(j < ATT_WIN, j, j - span)
    idx = np.clip(delta - B_PAST_CHUNKS * CHUNK, -REL_CLIP, CHUNK - 1) + REL_CLIP
    e = (rel_bias.astype(f32) * LOG2_E)[jnp.asarray(idx)].T
    flat = jnp.tile(e, (1, ATT_TILE))[:, :ATT_TILE * (span - 1)]
    table = flat.reshape(n_heads, ATT_TILE, span - 1)[:, :, :ATT_WIN]
    qchunk = np.arange(ATT_TILE)[:, None] // CHUNK + B_PAST_CHUNKS
    back = qchunk - np.arange(ATT_WIN)[None, :] // CHUNK
    in_band = (back >= 0) & (back <= B_PAST_CHUNKS)
    return jnp.swapaxes(jnp.where(jnp.asarray(in_band)[None], table, MASK_VALUE), 1, 2)


def _attn_layer(h, gain, mods, layer, w_q, w_o, w_layer, k, v_t, rel_bias):
    bsz, seq, d = h.shape
    assert seq % ATT_TILE == 0 and d == B_HEADS * B_HEAD_DIM
    bias = _attn_bias_table(rel_bias)
    h_spec = pl.BlockSpec((None, ATT_TILE, d), lambda b, i: (b, i, 0))

    def k_spec(back):
        return pl.BlockSpec((None, ATT_TILE, d), lambda b, i: (b, jnp.maximum(i - back, 0), 0))

    def vt_spec(back):
        return pl.BlockSpec((None, d, ATT_TILE), lambda b, i: (b, 0, jnp.maximum(i - back, 0)))

    return pl.pallas_call(
        _attn_layer_kernel,
        grid=(bsz, seq // ATT_TILE),
        in_specs=[
            h_spec,
            pl.BlockSpec((None, ATT_TILE, d), lambda b, i: (b, jnp.minimum(i + 1, seq // ATT_TILE - 1), 0)),
            _row_spec(d), _mod_spec(layer, 0, d), _mod_spec(layer, 1, d),
            _mod_spec(layer, 2, d),
            _resident((None, d, d), lambda b, i: (w_layer, 0, 0)),
            _resident((None, d, d), lambda b, i: (w_layer, 0, 0)),
            k_spec(2), k_spec(1), k_spec(0),
            vt_spec(2), vt_spec(1), vt_spec(0),
            _resident((B_HEADS, ATT_WIN, ATT_TILE), lambda b, i: (0, 0, 0)),
        ],
        out_specs=h_spec,
        out_shape=jax.ShapeDtypeStruct((bsz, seq, d), f32),
        scratch_shapes=[
            pltpu.VMEM((ATT_TILE, d), bf16),
            pltpu.VMEM((ATT_TILE, d), bf16),
            pltpu.VMEM((B_HEADS, ATT_WIN, ATT_TILE), f32),
            pltpu.VMEM((d, ATT_TILE), f32),
        ],
        compiler_params=_params("parallel", "arbitrary"),
        name="attn_layer",
    )(h, h, gain.reshape(1, d), mods, mods, mods, w_q, w_o, k, k, k, v_t, v_t, v_t, bias)


def kernel(x, c, mod_w, mod_b, norm_mix, norm_ffn, ffn_w_in, ffn_w_out, a_w_in, a_w_out, a_lb,
           a_out_norm, kv_norm, kv_mod_w, kv_mod_b, kv_w, b_w_q, b_w_o, b_rel_bias, final_norm):
    depth = mod_w.shape[0]
    n_a = a_w_in.shape[0]
    bsz = x.shape[0]

    mods = _modulation(c, mod_w, mod_b).reshape(depth, bsz, 1, -1)
    kv_mods = _modulation(c, kv_mod_w[None], kv_mod_b[None]).reshape(1, bsz, 1, -1)

    ffn_w_in_b = ffn_w_in.astype(bf16)
    ffn_w_out_b = ffn_w_out.astype(bf16)
    a_w_in_b = a_w_in.astype(bf16)
    a_w_out_b = a_w_out.astype(bf16)
    kv_w_b = kv_w.astype(bf16)
    b_w_q_b = b_w_q.astype(bf16)
    b_w_o_b = b_w_o.astype(bf16)

    h = x
    k = v_t = None
    for layer in range(depth):
        if layer < n_a:
            h = _hgrn_layer(h, norm_mix[layer], mods, layer, a_w_in_b, a_w_out_b, a_lb,
                            a_out_norm[layer])
        else:
            j = layer - n_a
            h = _attn_layer(h, norm_mix[layer], mods, layer, b_w_q_b, b_w_o_b, j, k, v_t,
                            b_rel_bias[j])
        if layer == n_a - 1:
            h, k, v_t = _ffn(h, norm_ffn[layer], mods, layer, ffn_w_in_b, ffn_w_out_b,
                         kv=(kv_norm, kv_mods, kv_w_b))
        elif layer == depth - 1:
            h = _ffn(h, norm_ffn[layer], mods, layer, ffn_w_in_b, ffn_w_out_b, final_gain=final_norm)
        else:
            h = _ffn(h, norm_ffn[layer], mods, layer, ffn_w_in_b, ffn_w_out_b)
    return h
```

```python
import functools

import numpy as np
import jax
import jax.numpy as jnp
from jax import lax
from jax.experimental import pallas as pl
from jax.experimental.pallas import tpu as pltpu

V7X_LANES = 128
V7X_SUBLANES = 8
V7X_MXU_DIM = 256
V7X_VMEM_LIMIT_BYTES = 56 * 1024 * 1024

CHUNK = 64
A_HEADS = 8
A_KEY_DIM = 128
B_HEADS = 16
B_HEAD_DIM = 64
B_PAST_CHUNKS = 8
REL_CLIP = 256
NORM_EPS = 1e-6
N_MOD = 6
MASK_VALUE = -1e30
MIN_FORGET = 1e-30
LOG2_E = 1.4426950408889634

SUB = V7X_SUBLANES
BLOCKS_PER_CHUNK = CHUNK // SUB
PAIR = 2 * A_KEY_DIM
N_OFF_PAIRS = BLOCKS_PER_CHUNK * (BLOCKS_PER_CHUNK - 1) // 2

HGRN_TILE = 256
ATT_TILE = 256
ATT_WIN = ATT_TILE + B_PAST_CHUNKS * CHUNK
ATT_GROUP = 4
ATT_GROUP_LANES = ATT_GROUP * B_HEAD_DIM
ATT_STAGE_HEADS = 16
ATT_KEY_BLOCK = 128
ATT_ONES_ROWS = 16
ROW_TILE = 512
MOD_MAX_TILE = 2048

f32 = jnp.float32
bf16 = jnp.bfloat16


def _params(*sem):
    return pltpu.CompilerParams(dimension_semantics=sem, vmem_limit_bytes=V7X_VMEM_LIMIT_BYTES)


def _resident(shape, index_map):
    return pl.BlockSpec(shape, index_map, pipeline_mode=pl.Buffered(1))


def _row_spec(d):
    return pl.BlockSpec((1, d), lambda b, i: (0, 0))


def _mod_spec(layer, col, d):
    return pl.BlockSpec((None, None, 1, d), lambda b, i: (layer, b, 0, col))


def _sigmoid(x):
    return 0.5 * jnp.tanh(0.5 * x) + 0.5


def _norm_mod(x, gain, shift, scale):
    y = x * lax.rsqrt(jnp.mean(x * x, axis=-1, keepdims=True) + NORM_EPS)
    return (y * gain) * (1.0 + scale) + shift


def _mod_kernel(c_ref, w_ref, b_ref, o_ref):
    c = c_ref[...]
    c_act = c * _sigmoid(c)
    o_ref[...] = jnp.dot(c_act, w_ref[...], preferred_element_type=f32,
                         precision=lax.Precision.HIGHEST) + b_ref[...]


def _modulation(c, w, b):
    n_layers, d, n = w.shape
    bsz = c.shape[0]
    tn = max(t for t in range(V7X_LANES, MOD_MAX_TILE + 1, V7X_LANES) if n % t == 0)
    return pl.pallas_call(
        _mod_kernel,
        grid=(n_layers, n // tn),
        in_specs=[
            pl.BlockSpec((bsz, d), lambda l, j: (0, 0)),
            pl.BlockSpec((None, d, tn), lambda l, j: (l, 0, j)),
            pl.BlockSpec((None, 1, tn), lambda l, j: (l, 0, j)),
        ],
        out_specs=pl.BlockSpec((None, bsz, tn), lambda l, j: (l, 0, j)),
        out_shape=jax.ShapeDtypeStruct((n_layers, bsz, n), f32),
        compiler_params=_params("arbitrary", "arbitrary"),
        name="modulation",
    )(c, w, b.reshape(n_layers, 1, n))


def _ffn_kernel(*refs, ffn_dim, final_norm, with_kv):
    h_ref, gain_ref, shift_ref, scale_ref, gate_ref, win_ref, wout_ref = refs[:7]
    rest = refs[7:]
    x = h_ref[...]
    u = _norm_mod(x, gain_ref[...], shift_ref[...], scale_ref[...]).astype(bf16)
    mid = (ffn_dim // 2) // V7X_MXU_DIM * V7X_MXU_DIM
    acc = None
    for lo, hi in ((0, mid), (mid, ffn_dim)):
        a = jnp.dot(u, win_ref[:, lo:hi], preferred_element_type=f32)
        b = jnp.dot(u, win_ref[:, ffn_dim + lo:ffn_dim + hi], preferred_element_type=f32)
        act = ((a * _sigmoid(a)) * b).astype(bf16)
        p = jnp.dot(act, wout_ref[lo:hi, :], preferred_element_type=f32)
        acc = p if acc is None else acc + p
    out = x + gate_ref[...] * acc
    if with_kv:
        kgain_ref, kshift_ref, kscale_ref, kvw_ref, o_ref, k_ref, vt_ref = rest
        ukv = _norm_mod(out, kgain_ref[...], kshift_ref[...], kscale_ref[...]).astype(bf16)
        kv = jnp.dot(ukv, kvw_ref[...], preferred_element_type=f32)
        d_kv = k_ref.shape[-1]
        k_ref[...] = kv[:, :d_kv].astype(k_ref.dtype)
        vt_ref[...] = kv[:, d_kv:].T.astype(vt_ref.dtype)
    elif final_norm:
        fgain_ref, o_ref = rest
        out = out * lax.rsqrt(jnp.mean(out * out, axis=-1, keepdims=True) + NORM_EPS)
        out = out * fgain_ref[...]
    else:
        (o_ref,) = rest
    o_ref[...] = out


def _ffn(h, gain, mods, layer, w_in, w_out, final_gain=None, kv=None):
    bsz, seq, d = h.shape
    ffn_dim = w_out.shape[1]
    tm = min(ROW_TILE, seq)
    h_spec = pl.BlockSpec((None, tm, d), lambda b, i: (b, i, 0))
    in_specs = [
        h_spec, _row_spec(d), _mod_spec(layer, 3, d), _mod_spec(layer, 4, d), _mod_spec(layer, 5, d),
        _resident((None, d, 2 * ffn_dim), lambda b, i: (layer, 0, 0)),
        _resident((None, ffn_dim, d), lambda b, i: (layer, 0, 0)),
    ]
    args = [h, gain.reshape(1, d), mods, mods, mods, w_in, w_out]
    out_specs = h_spec
    out_shape = jax.ShapeDtypeStruct((bsz, seq, d), f32)
    if kv is not None:
        kv_norm, kv_mods, kv_w = kv
        n_kv = kv_w.shape[-1]
        in_specs += [_row_spec(d), _mod_spec(0, 0, d), _mod_spec(0, 1, d),
                     _resident((d, n_kv), lambda b, i: (0, 0))]
        args += [kv_norm.reshape(1, d), kv_mods, kv_mods, kv_w]
        out_specs = (h_spec, pl.BlockSpec((None, tm, n_kv // 2), lambda b, i: (b, i, 0)),
                     pl.BlockSpec((None, n_kv // 2, tm), lambda b, i: (b, 0, i)))
        out_shape = (out_shape, jax.ShapeDtypeStruct((bsz, seq, n_kv // 2), bf16),
                     jax.ShapeDtypeStruct((bsz, n_kv // 2, seq), bf16))
    elif final_gain is not None:
        in_specs += [_row_spec(d)]
        args += [final_gain.reshape(1, d)]
    return pl.pallas_call(
        functools.partial(_ffn_kernel, ffn_dim=ffn_dim, final_norm=final_gain is not None,
                          with_kv=kv is not None),
        grid=(bsz, seq // tm),
        in_specs=in_specs,
        out_specs=out_specs,
        out_shape=out_shape,
        compiler_params=_params("parallel", "arbitrary"),
        name="ffn",
    )(*args)


def _block_diag2(a, b):
    z = jnp.zeros_like(a)
    return jnp.concatenate([jnp.concatenate([a, z], axis=1),
                            jnp.concatenate([z, b], axis=1)], axis=0)


def _block_cumsum(x3):
    sub = lax.broadcasted_iota(jnp.int32, x3.shape, 1)
    for sh in (1, 2, 4):
        x3 = x3 + jnp.where(sub >= sh, pltpu.roll(x3, sh, axis=1), 0.0)
    return x3


def _hgrn_layer_kernel(h_ref, hnext_ref, gain_ref, shift_ref, scale_ref, gate_ref, win_ref, wout_ref,
                       alb_ref, ogain_ref, wsel_ref, wselw_ref, ones_ref, o_ref,
                       st_ref, y_scr, q_scr, w_scr, c_scr, qt_scr, kt_scr, vb_scr, sg_scr, ctmp_scr,
                       ctot_scr, ginc_scr, lhs_scr, diag_scr, o_scr, *, layer, tile):
    d = A_HEADS * A_KEY_DIM
    nb = tile // SUB
    n_chunks = tile // CHUNK
    n_pairs = A_HEADS // 2

    def in_proj(src_ref):
        u = _norm_mod(src_ref[...], gain_ref[...], shift_ref[...], scale_ref[...]).astype(bf16)
        y_scr[...] = jnp.dot(u, win_ref[...], preferred_element_type=f32)

    @pl.when(pl.program_id(1) == 0)
    def _():
        st_ref[...] = jnp.zeros_like(st_ref)
        in_proj(h_ref)

    a_lb = alb_ref[...]
    e_lb = jnp.exp(a_lb - jnp.max(a_lb, axis=0, keepdims=True))
    sm = e_lb / jnp.sum(e_lb, axis=0, keepdims=True)
    lb = jnp.zeros((1, d), f32)
    for m in range(1, layer + 1):
        lb = lb + sm[m:m + 1, :]
    oml = 1.0 - lb

    qpre = y_scr[:, 0:d]
    zf = y_scr[:, d:2 * d]
    q = qpre * _sigmoid(qpre)
    sig = _sigmoid(zf)
    log_f = jnp.log2(jnp.maximum(lb + oml * sig, MIN_FORGET))
    k_sign = jnp.where(oml < 0.0, -1.0, 1.0)
    log_k = jnp.log2(jnp.abs(oml) * (1.0 - sig))

    c3 = _block_cumsum(log_f.reshape(nb, SUB, d))
    ctot3 = c3[:, SUB - 1:SUB, :]
    w3 = c3 - log_k.reshape(nb, SUB, d)
    q_scr[...] = (q * k_sign).astype(bf16)
    c_scr[...] = c3.reshape(tile, d)
    w_scr[...] = w3.reshape(tile, d)
    qt_scr[...] = (q.reshape(nb, SUB, d) * jnp.exp2(c3)).reshape(tile, d)
    kt_scr[...] = (jnp.exp2(ctot3 - w3).reshape(tile, d)) * k_sign
    vb_scr[...] = y_scr[:, 2 * d:3 * d].astype(bf16)
    g = y_scr[:, 3 * d:4 * d]
    sg_scr[...] = g * _sigmoid(g)
    in_proj(hnext_ref)

    for h in range(d // V7X_LANES):
        ctmp_scr[h] = c_scr[:, h * V7X_LANES:(h + 1) * V7X_LANES]
    ctot = jnp.concatenate([ctmp_scr[h, pl.ds(SUB - 1, nb, stride=SUB), :]
                            for h in range(d // V7X_LANES)], axis=1)
    ctot_scr[...] = ctot
    ginc_scr[...] = _block_cumsum(ctot.reshape(n_chunks, BLOCKS_PER_CHUNK, d)).reshape(nb, d)

    sub = lax.broadcasted_iota(jnp.int32, (nb, SUB, d), 1)
    for s in range(SUB):
        c3 = c_scr[...].reshape(nb, SUB, d)
        w_s = w_scr[...].reshape(nb, SUB, d)[:, s:s + 1, :]
        decay = jnp.exp2(jnp.where(sub >= s, c3 - w_s, MASK_VALUE))
        p = decay.reshape(tile, d).astype(bf16) * q_scr[...]
        for m in range(n_pairs):
            lhs_scr[m * tile:(m + 1) * tile, s * PAIR:(s + 1) * PAIR] = p[:, m * PAIR:(m + 1) * PAIR]
    half = n_pairs * tile // 2
    diag_scr[0:half] = jnp.dot(lhs_scr[0:half], wsel_ref[...], preferred_element_type=f32)
    diag_scr[half:] = jnp.dot(lhs_scr[half:], wselw_ref[...],
                              preferred_element_type=f32)[:, :2 * CHUNK]

    lane = lax.broadcasted_iota(jnp.int32, (SUB, V7X_LANES), 1)
    row = lax.broadcasted_iota(jnp.int32, (SUB, V7X_LANES), 0)
    col_block = (lane % CHUNK) // SUB
    col_sub = lane % SUB

    units = [(ci, m) for ci in range(n_chunks) for m in range(n_pairs)]
    per_chunk = []
    for ci in range(n_chunks):
        r0, b0 = ci * CHUNK, ci * BLOCKS_PER_CHUNK
        qt = qt_scr[r0:r0 + CHUNK, :]
        kt = kt_scr[r0:r0 + CHUNK, :]
        ginc = ginc_scr[b0:b0 + BLOCKS_PER_CHUNK, :]
        gexc = ginc - ctot_scr[b0:b0 + BLOCKS_PER_CHUNK, :]
        btot = ginc[BLOCKS_PER_CHUNK - 1:BLOCKS_PER_CHUNK, :]
        e_g = jnp.exp2(gexc)
        e_r = jnp.exp2(btot - ginc)
        qh = jnp.concatenate([qt[SUB * i:SUB * (i + 1)] * e_g[i:i + 1] for i in range(BLOCKS_PER_CHUNK)],
                             axis=0).astype(bf16)
        kh = jnp.concatenate([kt[SUB * i:SUB * (i + 1)] * e_r[i:i + 1] for i in range(BLOCKS_PER_CHUNK)],
                             axis=0).astype(bf16)
        rows = []
        for i in range(1, BLOCKS_PER_CHUNK):
            d_i = jnp.exp2(gexc[i:i + 1] - ginc[0:i])
            for j in range(i):
                rows.append(qt[SUB * i:SUB * (i + 1)] * d_i[j:j + 1])
        lhs_off = jnp.concatenate(rows, axis=0).astype(bf16)
        per_chunk.append(dict(qh=qh, kh=kh, lhs_off=lhs_off, ktb=kt.astype(bf16),
                              vb=vb_scr[r0:r0 + CHUNK, :], e_b=jnp.exp2(btot)))

    def pair_lanes(m):
        return m * PAIR, m * PAIR + A_KEY_DIM, (m + 1) * PAIR

    offs = {}
    for ci, m in units:
        lo, mid, hi = pair_lanes(m)
        c = per_chunk[ci]
        k_bd = _block_diag2(c["ktb"][:, lo:mid], c["ktb"][:, mid:hi])
        offs[ci, m] = lax.dot_general(c["lhs_off"][:, lo:hi], k_bd, (((1,), (1,)), ((), ())),
                                      preferred_element_type=f32)
    a2s = {}
    for ci, m in units:
        off = offs[ci, m]
        dg = diag_scr[m * tile + ci * CHUNK:m * tile + (ci + 1) * CHUNK, :]
        blocks = []
        for i in range(BLOCKS_PER_CHUNK):
            a = jnp.where(col_block == i, jnp.where(col_sub <= row, dg[SUB * i:SUB * (i + 1)], 0.0), 0.0)
            for j in range(i):
                pidx = i * (i - 1) // 2 + j
                a = jnp.where(col_block == j, off[SUB * pidx:SUB * (pidx + 1)], a)
            blocks.append(a)
        a2s[ci, m] = jnp.concatenate(blocks, axis=0).astype(bf16)
    for ci, m in units:
        lo, mid, hi = pair_lanes(m)
        vb = per_chunk[ci]["vb"]
        v_bd = _block_diag2(vb[:, lo:mid], vb[:, mid:hi])
        o_scr[ci * CHUNK:(ci + 1) * CHUNK, lo:hi] = jnp.dot(a2s[ci, m], v_bd,
                                                            preferred_element_type=f32)

    for ci in range(n_chunks):
        c = per_chunk[ci]
        for m in range(n_pairs):
            lo, mid, hi = pair_lanes(m)
            st_e = st_ref[2 * m]
            st_o = st_ref[2 * m + 1]
            st_bd = _block_diag2(st_e.astype(bf16), st_o.astype(bf16))
            o_inter = lax.dot_general(c["qh"][:, lo:hi], st_bd, (((1,), (1,)), ((), ())),
                                      preferred_element_type=f32)
            o_scr[ci * CHUNK:(ci + 1) * CHUNK, lo:hi] += o_inter
            upd = lax.dot_general(c["vb"][:, lo:hi], c["kh"][:, lo:hi], (((0,), (0,)), ((), ())),
                                  preferred_element_type=f32)
            st_ref[2 * m] = c["e_b"][:, lo:mid] * st_e + upd[:A_KEY_DIM, :A_KEY_DIM]
            st_ref[2 * m + 1] = c["e_b"][:, mid:hi] * st_o + upd[A_KEY_DIM:, A_KEY_DIM:]

    o = o_scr[...]
    osq = (o * o).astype(bf16)
    ms = jnp.concatenate(
        [jnp.dot(osq[:, m * PAIR:(m + 1) * PAIR], ones_ref[...], preferred_element_type=f32)
         for m in range(n_pairs)], axis=1) * (1.0 / A_KEY_DIM)
    out = o * lax.rsqrt(ms + NORM_EPS) * ogain_ref[...]
    out = (out * sg_scr[...]).astype(bf16)
    mix = jnp.dot(out, wout_ref[...], preferred_element_type=f32)
    o_ref[...] = h_ref[...] + gate_ref[...] * mix


def _hgrn_selector():
    r = np.arange(SUB * PAIR)
    s_local, h_r = r // PAIR, (r % PAIR) // A_KEY_DIM
    c = np.arange(2 * CHUNK)
    h_c, s_c = c // CHUNK, c % CHUNK
    sel = (h_r[:, None] == h_c[None, :]) & (s_local[:, None] == (s_c % SUB)[None, :])
    return jnp.asarray(sel, dtype=bf16)


def _hgrn_layer(h, gain, mods, layer, w_in, w_out, a_lb, o_gain):
    bsz, seq, d = h.shape
    assert d == A_HEADS * A_KEY_DIM
    tile = min(HGRN_TILE, seq)
    n_pairs = A_HEADS // 2
    n_in = w_in.shape[-1]
    ones_bd = jnp.asarray(np.kron(np.eye(2), np.ones((A_KEY_DIM, A_KEY_DIM))), dtype=bf16)
    gain_t = jnp.tile(o_gain.astype(f32), A_HEADS).reshape(1, d)
    n_a = a_lb.shape[0]
    sel = _hgrn_selector()
    h_spec = pl.BlockSpec((None, tile, d), lambda b, i: (b, i, 0))
    return pl.pallas_call(
        functools.partial(_hgrn_layer_kernel, layer=layer, tile=tile),
        grid=(bsz, seq // tile),
        in_specs=[
            h_spec,
            pl.BlockSpec((None, tile, d), lambda b, i: (b, jnp.minimum(i + 1, seq // tile - 1), 0)),
            _row_spec(d), _mod_spec(layer, 0, d), _mod_spec(layer, 1, d),
            _mod_spec(layer, 2, d),
            _resident((None, d, n_in), lambda b, i: (layer, 0, 0)),
            _resident((None, d, d), lambda b, i: (layer, 0, 0)),
            pl.BlockSpec((n_a, d), lambda b, i: (0, 0)),
            _row_spec(d),
            _resident((SUB * PAIR, 2 * CHUNK), lambda b, i: (0, 0)),
            _resident((SUB * PAIR, 4 * CHUNK), lambda b, i: (0, 0)),
            _resident((PAIR, PAIR), lambda b, i: (0, 0)),
        ],
        out_specs=h_spec,
        out_shape=jax.ShapeDtypeStruct((bsz, seq, d), f32),
        scratch_shapes=[
            pltpu.VMEM((A_HEADS, A_KEY_DIM, A_KEY_DIM), f32),
            pltpu.VMEM((tile, n_in), f32),
            pltpu.VMEM((tile, d), bf16),
            pltpu.VMEM((tile, d), f32),
            pltpu.VMEM((tile, d), f32),
            pltpu.VMEM((tile, d), f32),
            pltpu.VMEM((tile, d), f32),
            pltpu.VMEM((tile, d), bf16),
            pltpu.VMEM((tile, d), f32),
            pltpu.VMEM((d // V7X_LANES, tile, V7X_LANES), f32),
            pltpu.VMEM((tile // SUB, d), f32),
            pltpu.VMEM((tile // SUB, d), f32),
            pltpu.VMEM((n_pairs * tile, SUB * PAIR), bf16),
            pltpu.VMEM((n_pairs * tile, 2 * CHUNK), f32),
            pltpu.VMEM((tile, d), f32),
        ],
        compiler_params=_params("parallel", "arbitrary"),
        name="hgrn_layer",
    )(h, h, gain.reshape(1, d), mods, mods, mods, w_in, w_out, a_lb.astype(f32), gain_t,
      sel, jnp.concatenate([sel, jnp.zeros_like(sel)], axis=1), ones_bd)


def _attn_layer_kernel(h_ref, hnext_ref, gain_ref, shift_ref, scale_ref, gate_ref, wq_ref, wo_ref,
                       k0_ref, k1_ref, k2_ref, vt0_ref, vt1_ref, vt2_ref, bias_ref, o_ref,
                       q_scr, qnext_scr, biasp_scr, ot_scr):
    i = pl.program_id(1)

    def q_proj(src_ref):
        u = _norm_mod(src_ref[...], gain_ref[...], shift_ref[...], scale_ref[...]).astype(bf16)
        q = jnp.dot(u, wq_ref[...], preferred_element_type=f32) * (B_HEAD_DIM ** -0.5 * LOG2_E)
        qnext_scr[...] = q.astype(bf16)

    @pl.when(i == 0)
    def _():
        q_proj(h_ref)

    @pl.when(i <= ATT_WIN // ATT_TILE - 1)
    def _():
        krow = lax.broadcasted_iota(jnp.int32, (ATT_WIN, ATT_TILE), 0)
        pad = jnp.where(krow >= (ATT_WIN // ATT_TILE - 1 - i) * ATT_TILE, 0.0, MASK_VALUE)
        for hd in range(B_HEADS):
            biasp_scr[hd] = bias_ref[hd] + pad

    q_scr[...] = qnext_scr[...]
    q_proj(hnext_ref)

    lane = lax.broadcasted_iota(jnp.int32, (1, ATT_GROUP_LANES), 1)
    head_of_lane = lane // B_HEAD_DIM
    ones_rows = jnp.ones((ATT_ONES_ROWS, ATT_WIN), bf16)
    for h0 in range(0, B_HEADS, ATT_STAGE_HEADS):
        heads = list(range(h0, h0 + ATT_STAGE_HEADS))
        s_ts = []
        for head in heads:
            g, hh = divmod(head, ATT_GROUP)
            lanes = slice(g * ATT_GROUP_LANES, (g + 1) * ATT_GROUP_LANES)
            kwin = jnp.concatenate([k0_ref[:, lanes], k1_ref[:, lanes], k2_ref[:, lanes]], axis=0)
            keep = jnp.where(head_of_lane == hh, 1.0, 0.0).astype(bf16)
            s_ts.append(lax.dot_general(kwin, q_scr[:, lanes] * keep, (((1,), (1,)), ((), ())),
                                        preferred_element_type=f32))
        partials = []
        for head, s_t in zip(heads, s_ts):
            rows = slice(head * B_HEAD_DIM, (head + 1) * B_HEAD_DIM)
            v_t = jnp.concatenate([vt0_ref[rows, :], vt1_ref[rows, :], vt2_ref[rows, :]], axis=1)
            v_ext = jnp.concatenate([v_t, ones_rows], axis=0)
            blocks = []
            for k0 in range(0, ATT_WIN, ATT_KEY_BLOCK):
                keys = slice(k0, k0 + ATT_KEY_BLOCK)
                s_b = s_t[keys, :] + biasp_scr[head, keys, :]
                m_b = jnp.max(s_b, axis=0, keepdims=True)
                p_b = jnp.exp2(s_b - m_b).astype(bf16)
                blocks.append((m_b, jnp.dot(v_ext[:, keys], p_b, preferred_element_type=f32)))
            partials.append(blocks)
        for head, blocks in zip(heads, partials):
            rows = slice(head * B_HEAD_DIM, (head + 1) * B_HEAD_DIM)
            m = blocks[0][0]
            for m_b, _ in blocks[1:]:
                m = jnp.maximum(m, m_b)
            o_t = None
            for m_b, o_b in blocks:
                term = o_b * jnp.exp2(m_b - m)
                o_t = term if o_t is None else o_t + term
            ot_scr[rows, :] = o_t[:B_HEAD_DIM] * (1.0 / o_t[B_HEAD_DIM:B_HEAD_DIM + 1])
    mix = lax.dot_general(ot_scr[...].astype(bf16), wo_ref[...], (((0,), (0,)), ((), ())),
                          preferred_element_type=f32)
    o_ref[...] = h_ref[...] + gate_ref[...] * mix


def _attn_bias_table(rel_bias):
    n_rel, n_heads = rel_bias.shape
    span = ATT_TILE + ATT_WIN
    j = np.arange(span)
    delta = np.where(j < ATT_WIN, j, j - span)
    idx = np.clip(delta - B_PAST_CHUNKS * CHUNK, -REL_CLIP, CHUNK - 1) + REL_CLIP
    e = (rel_bias.astype(f32) * LOG2_E)[jnp.asarray(idx)].T
    flat = jnp.tile(e, (1, ATT_TILE))[:, :ATT_TILE * (span - 1)]
    table = flat.reshape(n_heads, ATT_TILE, span - 1)[:, :, :ATT_WIN]
    qchunk = np.arange(ATT_TILE)[:, None] // CHUNK + B_PAST_CHUNKS
    back = qchunk - np.arange(ATT_WIN)[None, :] // CHUNK
    in_band = (back >= 0) & (back <= B_PAST_CHUNKS)
    return jnp.swapaxes(jnp.where(jnp.asarray(in_band)[None], table, MASK_VALUE), 1, 2)


def _attn_layer(h, gain, mods, layer, w_q, w_o, w_layer, k, v_t, rel_bias):
    bsz, seq, d = h.shape
    assert seq % ATT_TILE == 0 and d == B_HEADS * B_HEAD_DIM
    bias = _attn_bias_table(rel_bias)
    h_spec = pl.BlockSpec((None, ATT_TILE, d), lambda b, i: (b, i, 0))

    def k_spec(back):
        return pl.BlockSpec((None, ATT_TILE, d), lambda b, i: (b, jnp.maximum(i - back, 0), 0))

    def vt_spec(back):
        return pl.BlockSpec((None, d, ATT_TILE), lambda b, i: (b, 0, jnp.maximum(i - back, 0)))

    return pl.pallas_call(
        _attn_layer_kernel,
        grid=(bsz, seq // ATT_TILE),
        in_specs=[
            h_spec,
            pl.BlockSpec((None, ATT_TILE, d), lambda b, i: (b, jnp.minimum(i + 1, seq // ATT_TILE - 1), 0)),
            _row_spec(d), _mod_spec(layer, 0, d), _mod_spec(layer, 1, d),
            _mod_spec(layer, 2, d),
            _resident((None, d, d), lambda b, i: (w_layer, 0, 0)),
            _resident((None, d, d), lambda b, i: (w_layer, 0, 0)),
            k_spec(2), k_spec(1), k_spec(0),
            vt_spec(2), vt_spec(1), vt_spec(0),
            _resident((B_HEADS, ATT_WIN, ATT_TILE), lambda b, i: (0, 0, 0)),
        ],
        out_specs=h_spec,
        out_shape=jax.ShapeDtypeStruct((bsz, seq, d), f32),
        scratch_shapes=[
            pltpu.VMEM((ATT_TILE, d), bf16),
            pltpu.VMEM((ATT_TILE, d), bf16),
            pltpu.VMEM((B_HEADS, ATT_WIN, ATT_TILE), f32),
            pltpu.VMEM((d, ATT_TILE), f32),
        ],
        compiler_params=_params("parallel", "arbitrary"),
        name="attn_layer",
    )(h, h, gain.reshape(1, d), mods, mods, mods, w_q, w_o, k, k, k, v_t, v_t, v_t, bias)


def kernel(x, c, mod_w, mod_b, norm_mix, norm_ffn, ffn_w_in, ffn_w_out, a_w_in, a_w_out, a_lb,
           a_out_norm, kv_norm, kv_mod_w, kv_mod_b, kv_w, b_w_q, b_w_o, b_rel_bias, final_norm):
    depth = mod_w.shape[0]
    n_a = a_w_in.shape[0]
    bsz = x.shape[0]

    mods = _modulation(c, mod_w, mod_b).reshape(depth, bsz, 1, -1)
    kv_mods = _modulation(c, kv_mod_w[None], kv_mod_b[None]).reshape(1, bsz, 1, -1)

    ffn_w_in_b = ffn_w_in.astype(bf16)
    ffn_w_out_b = ffn_w_out.astype(bf16)
    a_w_in_b = a_w_in.astype(bf16)
    a_w_out_b = a_w_out.astype(bf16)
    kv_w_b = kv_w.astype(bf16)
    b_w_q_b = b_w_q.astype(bf16)
    b_w_o_b = b_w_o.astype(bf16)

    h = x
    k = v_t = None
    for layer in range(depth):
        if layer < n_a:
            h = _hgrn_layer(h, norm_mix[layer], mods, layer, a_w_in_b, a_w_out_b, a_lb,
                            a_out_norm[layer])
        else:
            j = layer - n_a
            h = _attn_layer(h, norm_mix[layer], mods, layer, b_w_q_b, b_w_o_b, j, k, v_t,
                            b_rel_bias[j])
        if layer == n_a - 1:
            h, k, v_t = _ffn(h, norm_ffn[layer], mods, layer, ffn_w_in_b, ffn_w_out_b,
                         kv=(kv_norm, kv_mods, kv_w_b))
        elif layer == depth - 1:
            h = _ffn(h, norm_ffn[layer], mods, layer, ffn_w_in_b, ffn_w_out_b, final_gain=final_norm)
        else:
            h = _ffn(h, norm_ffn[layer], mods, layer, ffn_w_in_b, ffn_w_out_b)
    return h
```

```python
import functools

import numpy as np
import jax
import jax.numpy as jnp
from jax import lax
from jax.experimental import pallas as pl
from jax.experimental.pallas import tpu as pltpu

V7X_LANES = 128
V7X_SUBLANES = 8
V7X_MXU_DIM = 256
V7X_VMEM_LIMIT_BYTES = 56 * 1024 * 1024

CHUNK = 64
A_HEADS = 8
A_KEY_DIM = 128
B_HEADS = 16
B_HEAD_DIM = 64
B_PAST_CHUNKS = 8
REL_CLIP = 256
NORM_EPS = 1e-6
MASK_VALUE = -1e30
MIN_FORGET = 1e-30
LOG2_E = 1.4426950408889634

SUB = V7X_SUBLANES
BLOCKS_PER_CHUNK = CHUNK // SUB
PAIR = 2 * A_KEY_DIM

HGRN_TILE = 512
ATT_TILE = 256
ATT_WIN = ATT_TILE + B_PAST_CHUNKS * CHUNK
ATT_GROUP = 4
ATT_GROUP_LANES = ATT_GROUP * B_HEAD_DIM
ATT_STAGE_HEADS = 16
ATT_KEY_BLOCK = 128
ATT_ONES_ROWS = 16
ROW_TILE = 1024
ROW_TILE_KV = 512
MOD_MAX_TILE = 2048

f32 = jnp.float32
bf16 = jnp.bfloat16


def _params(*sem):
    return pltpu.CompilerParams(dimension_semantics=sem, vmem_limit_bytes=V7X_VMEM_LIMIT_BYTES)


def _resident(shape, index_map):
    return pl.BlockSpec(shape, index_map, pipeline_mode=pl.Buffered(1))


def _row_spec(d):
    return pl.BlockSpec((1, d), lambda b, i: (0, 0))


def _mod_spec(layer, col, d):
    return pl.BlockSpec((None, None, 1, d), lambda b, i: (layer, b, 0, col))


def _sigmoid(x):
    return 0.5 * jnp.tanh(0.5 * x) + 0.5


def _norm_mod(x, gain, shift, scale):
    y = x * lax.rsqrt(jnp.mean(x * x, axis=-1, keepdims=True) + NORM_EPS)
    return (y * gain) * (1.0 + scale) + shift


def _mod_kernel(c_ref, w_ref, b_ref, o_ref):
    c = c_ref[...]
    c_act = c * _sigmoid(c)
    o_ref[...] = jnp.dot(c_act, w_ref[...], preferred_element_type=f32,
                         precision=lax.Precision.HIGHEST) + b_ref[...]


def _modulation(c, w, b):
    n_layers, d, n = w.shape
    bsz = c.shape[0]
    tn = max(t for t in range(V7X_LANES, MOD_MAX_TILE + 1, V7X_LANES) if n % t == 0)
    return pl.pallas_call(
        _mod_kernel,
        grid=(n_layers, n // tn),
        in_specs=[
            pl.BlockSpec((bsz, d), lambda l, j: (0, 0)),
            pl.BlockSpec((None, d, tn), lambda l, j: (l, 0, j)),
            pl.BlockSpec((None, 1, tn), lambda l, j: (l, 0, j)),
        ],
        out_specs=pl.BlockSpec((None, bsz, tn), lambda l, j: (l, 0, j)),
        out_shape=jax.ShapeDtypeStruct((n_layers, bsz, n), f32),
        compiler_params=_params("arbitrary", "arbitrary"),
        name="modulation",
    )(c, w, b.reshape(n_layers, 1, n))


def _ffn_kernel(*refs, ffn_dim, final_norm, with_kv):
    h_ref, gain_ref, shift_ref, scale_ref, gate_ref, win_ref, wout_ref = refs[:7]
    rest = refs[7:]
    x = h_ref[...]
    u = _norm_mod(x, gain_ref[...], shift_ref[...], scale_ref[...]).astype(bf16)
    mid = (ffn_dim // 2) // V7X_MXU_DIM * V7X_MXU_DIM
    acc = None
    for lo, hi in ((0, mid), (mid, ffn_dim)):
        a = jnp.dot(u, win_ref[:, lo:hi], preferred_element_type=f32)
        b = jnp.dot(u, win_ref[:, ffn_dim + lo:ffn_dim + hi], preferred_element_type=f32)
        act = ((a * _sigmoid(a)) * b).astype(bf16)
        p = jnp.dot(act, wout_ref[lo:hi, :], preferred_element_type=f32)
        acc = p if acc is None else acc + p
    out = x + gate_ref[...] * acc
    if with_kv:
        kgain_ref, kshift_ref, kscale_ref, kvw_ref, o_ref, k_ref, vt_ref = rest
        ukv = _norm_mod(out, kgain_ref[...], kshift_ref[...], kscale_ref[...]).astype(bf16)
        kv = jnp.dot(ukv, kvw_ref[...], preferred_element_type=f32)
        d_kv = k_ref.shape[-1]
        k_ref[...] = kv[:, :d_kv].astype(k_ref.dtype)
        vt_ref[...] = kv[:, d_kv:].T.astype(vt_ref.dtype)
    elif final_norm:
        fgain_ref, o_ref = rest
        out = out * lax.rsqrt(jnp.mean(out * out, axis=-1, keepdims=True) + NORM_EPS)
        out = out * fgain_ref[...]
    else:
        (o_ref,) = rest
    o_ref[...] = out


def _ffn(h, gain, mods, layer, w_in, w_out, final_gain=None, kv=None):
    bsz, seq, d = h.shape
    ffn_dim = w_out.shape[1]
    tm = min(ROW_TILE if kv is None else ROW_TILE_KV, seq)
    assert seq % tm == 0 and ffn_dim % V7X_LANES == 0
    h_spec = pl.BlockSpec((None, tm, d), lambda b, i: (b, i, 0))
    in_specs = [
        h_spec, _row_spec(d), _mod_spec(layer, 3, d), _mod_spec(layer, 4, d), _mod_spec(layer, 5, d),
        _resident((None, d, 2 * ffn_dim), lambda b, i: (layer, 0, 0)),
        _resident((None, ffn_dim, d), lambda b, i: (layer, 0, 0)),
    ]
    args = [h, gain.reshape(1, d), mods, mods, mods, w_in, w_out]
    out_specs = h_spec
    out_shape = jax.ShapeDtypeStruct((bsz, seq, d), f32)
    if kv is not None:
        kv_norm, kv_mods, kv_w = kv
        n_kv = kv_w.shape[-1]
        in_specs += [_row_spec(d), _mod_spec(0, 0, d), _mod_spec(0, 1, d),
                     _resident((d, n_kv), lambda b, i: (0, 0))]
        args += [kv_norm.reshape(1, d), kv_mods, kv_mods, kv_w]
        out_specs = (h_spec, pl.BlockSpec((None, tm, n_kv // 2), lambda b, i: (b, i, 0)),
                     pl.BlockSpec((None, n_kv // 2, tm), lambda b, i: (b, 0, i)))
        out_shape = (out_shape, jax.ShapeDtypeStruct((bsz, seq, n_kv // 2), bf16),
                     jax.ShapeDtypeStruct((bsz, n_kv // 2, seq), bf16))
    elif final_gain is not None:
        in_specs += [_row_spec(d)]
        args += [final_gain.reshape(1, d)]
    return pl.pallas_call(
        functools.partial(_ffn_kernel, ffn_dim=ffn_dim, final_norm=final_gain is not None,
                          with_kv=kv is not None),
        grid=(bsz, seq // tm),
        in_specs=in_specs,
        out_specs=out_specs,
        out_shape=out_shape,
        compiler_params=_params("parallel", "arbitrary"),
        name="ffn",
    )(*args)


def _block_diag2(a, b):
    z = jnp.zeros_like(a)
    return jnp.concatenate([jnp.concatenate([a, z], axis=1),
                            jnp.concatenate([z, b], axis=1)], axis=0)


def _block_cumsum(x3):
    sub = lax.broadcasted_iota(jnp.int32, x3.shape, 1)
    for sh in (1, 2, 4):
        x3 = x3 + jnp.where(sub >= sh, pltpu.roll(x3, sh, axis=1), 0.0)
    return x3


def _hgrn_layer_kernel(h_ref, hnext_ref, gain_ref, shift_ref, scale_ref, gate_ref, win_ref, wout_ref,
                       alb_ref, ogain_ref, wsel_ref, wselw_ref, ones_ref, o_ref,
                       st_ref, y_scr, q_scr, w_scr, c_scr, qt_scr, kt_scr, vb_scr, sg_scr, ctmp_scr,
                       ctot_scr, ginc_scr, lhs_scr, diag_scr, o_scr, *, layer, tile):
    d = A_HEADS * A_KEY_DIM
    nb = tile // SUB
    n_chunks = tile // CHUNK
    n_pairs = A_HEADS // 2

    def in_proj(src_ref):
        u = _norm_mod(src_ref[...], gain_ref[...], shift_ref[...], scale_ref[...]).astype(bf16)
        y_scr[...] = jnp.dot(u, win_ref[...], preferred_element_type=f32)

    @pl.when(pl.program_id(1) == 0)
    def _():
        st_ref[...] = jnp.zeros_like(st_ref)
        in_proj(h_ref)

    a_lb = alb_ref[...]
    e_lb = jnp.exp(a_lb - jnp.max(a_lb, axis=0, keepdims=True))
    sm = e_lb / jnp.sum(e_lb, axis=0, keepdims=True)
    lb = jnp.zeros((1, d), f32)
    for m in range(1, layer + 1):
        lb = lb + sm[m:m + 1, :]
    oml = 1.0 - lb

    qpre = y_scr[:, 0:d]
    zf = y_scr[:, d:2 * d]
    q = qpre * _sigmoid(qpre)
    sig = _sigmoid(zf)
    log_f = jnp.log2(jnp.maximum(lb + oml * sig, MIN_FORGET))
    k_sign = jnp.where(oml < 0.0, -1.0, 1.0)
    log_k = jnp.log2(jnp.abs(oml) * (1.0 - sig))

    c3 = _block_cumsum(log_f.reshape(nb, SUB, d))
    ctot3 = c3[:, SUB - 1:SUB, :]
    w3 = c3 - log_k.reshape(nb, SUB, d)
    q_scr[...] = (q * k_sign).astype(bf16)
    c_scr[...] = c3.reshape(tile, d)
    w_scr[...] = w3.reshape(tile, d)
    qt_scr[...] = (q.reshape(nb, SUB, d) * jnp.exp2(c3)).reshape(tile, d)
    kt_scr[...] = (jnp.exp2(ctot3 - w3).reshape(tile, d)) * k_sign
    vb_scr[...] = y_scr[:, 2 * d:3 * d].astype(bf16)
    g = y_scr[:, 3 * d:4 * d]
    sg_scr[...] = g * _sigmoid(g)
    in_proj(hnext_ref)

    for h in range(d // V7X_LANES):
        ctmp_scr[h] = c_scr[:, h * V7X_LANES:(h + 1) * V7X_LANES]
    ctot = jnp.concatenate([ctmp_scr[h, pl.ds(SUB - 1, nb, stride=SUB), :]
                            for h in range(d // V7X_LANES)], axis=1)
    ctot_scr[...] = ctot
    ginc_scr[...] = _block_cumsum(ctot.reshape(n_chunks, BLOCKS_PER_CHUNK, d)).reshape(nb, d)

    sub = lax.broadcasted_iota(jnp.int32, (nb, SUB, d), 1)
    for s in range(SUB):
        c3 = c_scr[...].reshape(nb, SUB, d)
        w_s = w_scr[...].reshape(nb, SUB, d)[:, s:s + 1, :]
        decay = jnp.exp2(c3 - w_s if s == 0 else jnp.where(sub >= s, c3 - w_s, MASK_VALUE))
        p = decay.reshape(tile, d).astype(bf16) * q_scr[...]
        for m in range(n_pairs):
            lhs_scr[m * tile:(m + 1) * tile, s * PAIR:(s + 1) * PAIR] = p[:, m * PAIR:(m + 1) * PAIR]
    half = n_pairs * tile // 2
    diag_scr[0:half] = jnp.dot(lhs_scr[0:half], wsel_ref[...], preferred_element_type=f32)
    diag_scr[half:] = jnp.dot(lhs_scr[half:], wselw_ref[...],
                              preferred_element_type=f32)[:, :2 * CHUNK]

    lane = lax.broadcasted_iota(jnp.int32, (SUB, V7X_LANES), 1)
    row = lax.broadcasted_iota(jnp.int32, (SUB, V7X_LANES), 0)
    col_block = (lane % CHUNK) // SUB
    col_sub = lane % SUB

    units = [(ci, m) for ci in range(n_chunks) for m in range(n_pairs)]
    per_chunk = []
    for ci in range(n_chunks):
        r0, b0 = ci * CHUNK, ci * BLOCKS_PER_CHUNK
        qt = qt_scr[r0:r0 + CHUNK, :]
        kt = kt_scr[r0:r0 + CHUNK, :]
        ginc = ginc_scr[b0:b0 + BLOCKS_PER_CHUNK, :]
        gexc = ginc - ctot_scr[b0:b0 + BLOCKS_PER_CHUNK, :]
        btot = ginc[BLOCKS_PER_CHUNK - 1:BLOCKS_PER_CHUNK, :]
        e_g = jnp.exp2(gexc)
        e_r = jnp.exp2(btot - ginc)
        qh = jnp.concatenate([qt[SUB * i:SUB * (i + 1)] * e_g[i:i + 1] for i in range(BLOCKS_PER_CHUNK)],
                             axis=0).astype(bf16)
        kh = jnp.concatenate([kt[SUB * i:SUB * (i + 1)] * e_r[i:i + 1] for i in range(BLOCKS_PER_CHUNK)],
                             axis=0).astype(bf16)
        rows = []
        for i in range(1, BLOCKS_PER_CHUNK):
            d_i = jnp.exp2(gexc[i:i + 1] - ginc[0:i])
            for j in range(i):
                rows.append(qt[SUB * i:SUB * (i + 1)] * d_i[j:j + 1])
        lhs_off = jnp.concatenate(rows, axis=0).astype(bf16)
        per_chunk.append(dict(qh=qh, kh=kh, lhs_off=lhs_off, ktb=kt.astype(bf16),
                              vb=vb_scr[r0:r0 + CHUNK, :], e_b=jnp.exp2(btot)))

    def pair_lanes(m):
        return m * PAIR, m * PAIR + A_KEY_DIM, (m + 1) * PAIR

    offs = {}
    for ci, m in units:
        lo, mid, hi = pair_lanes(m)
        c = per_chunk[ci]
        k_bd = _block_diag2(c["ktb"][:, lo:mid], c["ktb"][:, mid:hi])
        offs[ci, m] = lax.dot_general(c["lhs_off"][:, lo:hi], k_bd, (((1,), (1,)), ((), ())),
                                      preferred_element_type=f32)
    a2s = {}
    for ci, m in units:
        off = offs[ci, m]
        dg = diag_scr[m * tile + ci * CHUNK:m * tile + (ci + 1) * CHUNK, :]
        blocks = []
        for i in range(BLOCKS_PER_CHUNK):
            a = jnp.where(col_block == i, jnp.where(col_sub <= row, dg[SUB * i:SUB * (i + 1)], 0.0), 0.0)
            for j in range(i):
                pidx = i * (i - 1) // 2 + j
                a = jnp.where(col_block == j, off[SUB * pidx:SUB * (pidx + 1)], a)
            blocks.append(a)
        a2s[ci, m] = jnp.concatenate(blocks, axis=0).astype(bf16)
    for ci, m in units:
        lo, mid, hi = pair_lanes(m)
        vb = per_chunk[ci]["vb"]
        v_bd = _block_diag2(vb[:, lo:mid], vb[:, mid:hi])
        o_scr[ci * CHUNK:(ci + 1) * CHUNK, lo:hi] = jnp.dot(a2s[ci, m], v_bd,
                                                            preferred_element_type=f32)

    for ci in range(n_chunks):
        c = per_chunk[ci]
        for m in range(n_pairs):
            lo, mid, hi = pair_lanes(m)
            st_e = st_ref[2 * m]
            st_o = st_ref[2 * m + 1]
            st_bd = _block_diag2(st_e.astype(bf16), st_o.astype(bf16))
            o_inter = lax.dot_general(c["qh"][:, lo:hi], st_bd, (((1,), (1,)), ((), ())),
                                      preferred_element_type=f32)
            o_scr[ci * CHUNK:(ci + 1) * CHUNK, lo:hi] += o_inter
            upd = lax.dot_general(c["vb"][:, lo:hi], c["kh"][:, lo:hi], (((0,), (0,)), ((), ())),
                                  preferred_element_type=f32)
            st_ref[2 * m] = c["e_b"][:, lo:mid] * st_e + upd[:A_KEY_DIM, :A_KEY_DIM]
            st_ref[2 * m + 1] = c["e_b"][:, mid:hi] * st_o + upd[A_KEY_DIM:, A_KEY_DIM:]

    o = o_scr[...]
    osq = (o * o).astype(bf16)
    ms = jnp.concatenate(
        [jnp.dot(osq[:, m * PAIR:(m + 1) * PAIR], ones_ref[...], preferred_element_type=f32)
         for m in range(n_pairs)], axis=1) * (1.0 / A_KEY_DIM)
    out = o * lax.rsqrt(ms + NORM_EPS) * ogain_ref[...]
    out = (out * sg_scr[...]).astype(bf16)
    mix = jnp.dot(out, wout_ref[...], preferred_element_type=f32)
    o_ref[...] = h_ref[...] + gate_ref[...] * mix


def _hgrn_selector():
    r = np.arange(SUB * PAIR)
    s_local, h_r = r // PAIR, (r % PAIR) // A_KEY_DIM
    c = np.arange(2 * CHUNK)
    h_c, s_c = c // CHUNK, c % CHUNK
    sel = (h_r[:, None] == h_c[None, :]) & (s_local[:, None] == (s_c % SUB)[None, :])
    return jnp.asarray(sel, dtype=bf16)


def _hgrn_layer(h, gain, mods, layer, w_in, w_out, a_lb, o_gain):
    bsz, seq, d = h.shape
    tile = min(HGRN_TILE, seq)
    assert d == A_HEADS * A_KEY_DIM and seq % tile == 0 and tile % CHUNK == 0
    n_pairs = A_HEADS // 2
    n_in = w_in.shape[-1]
    ones_bd = jnp.asarray(np.kron(np.eye(2), np.ones((A_KEY_DIM, A_KEY_DIM))), dtype=bf16)
    gain_t = jnp.tile(o_gain.astype(f32), A_HEADS).reshape(1, d)
    n_a = a_lb.shape[0]
    sel = _hgrn_selector()
    h_spec = pl.BlockSpec((None, tile, d), lambda b, i: (b, i, 0))
    return pl.pallas_call(
        functools.partial(_hgrn_layer_kernel, layer=layer, tile=tile),
        grid=(bsz, seq // tile),
        in_specs=[
            h_spec,
            pl.BlockSpec((None, tile, d), lambda b, i: (b, jnp.minimum(i + 1, seq // tile - 1), 0)),
            _row_spec(d), _mod_spec(layer, 0, d), _mod_spec(layer, 1, d),
            _mod_spec(layer, 2, d),
            _resident((None, d, n_in), lambda b, i: (layer, 0, 0)),
            _resident((None, d, d), lambda b, i: (layer, 0, 0)),
            pl.BlockSpec((n_a, d), lambda b, i: (0, 0)),
            _row_spec(d),
            _resident((SUB * PAIR, 2 * CHUNK), lambda b, i: (0, 0)),
            _resident((SUB * PAIR, 4 * CHUNK), lambda b, i: (0, 0)),
            _resident((PAIR, PAIR), lambda b, i: (0, 0)),
        ],
        out_specs=h_spec,
        out_shape=jax.ShapeDtypeStruct((bsz, seq, d), f32),
        scratch_shapes=[
            pltpu.VMEM((A_HEADS, A_KEY_DIM, A_KEY_DIM), f32),
            pltpu.VMEM((tile, n_in), f32),
            pltpu.VMEM((tile, d), bf16),
            pltpu.VMEM((tile, d), f32),
            pltpu.VMEM((tile, d), f32),
            pltpu.VMEM((tile, d), f32),
            pltpu.VMEM((tile, d), f32),
            pltpu.VMEM((tile, d), bf16),
            pltpu.VMEM((tile, d), f32),
            pltpu.VMEM((d // V7X_LANES, tile, V7X_LANES), f32),
            pltpu.VMEM((tile // SUB, d), f32),
            pltpu.VMEM((tile // SUB, d), f32),
            pltpu.VMEM((n_pairs * tile, SUB * PAIR), bf16),
            pltpu.VMEM((n_pairs * tile, 2 * CHUNK), f32),
            pltpu.VMEM((tile, d), f32),
        ],
        compiler_params=_params("parallel", "arbitrary"),
        name="hgrn_layer",
    )(h, h, gain.reshape(1, d), mods, mods, mods, w_in, w_out, a_lb.astype(f32), gain_t,
      sel, jnp.concatenate([sel, jnp.zeros_like(sel)], axis=1), ones_bd)


def _attn_layer_kernel(h_ref, hnext_ref, gain_ref, shift_ref, scale_ref, gate_ref, wq_ref, wo_ref,
                       k0_ref, k1_ref, k2_ref, vt0_ref, vt1_ref, vt2_ref, bias_ref, o_ref,
                       q_scr, qnext_scr, biasp_scr, ot_scr):
    i = pl.program_id(1)

    def q_proj(src_ref):
        u = _norm_mod(src_ref[...], gain_ref[...], shift_ref[...], scale_ref[...]).astype(bf16)
        q = jnp.dot(u, wq_ref[...], preferred_element_type=f32) * (B_HEAD_DIM ** -0.5 * LOG2_E)
        qnext_scr[...] = q.astype(bf16)

    @pl.when(i == 0)
    def _():
        q_proj(h_ref)

    @pl.when(i <= ATT_WIN // ATT_TILE - 1)
    def _():
        krow = lax.broadcasted_iota(jnp.int32, (ATT_WIN, ATT_TILE), 0)
        pad = jnp.where(krow >= (ATT_WIN // ATT_TILE - 1 - i) * ATT_TILE, 0.0, MASK_VALUE)
        for hd in range(B_HEADS):
            biasp_scr[hd] = bias_ref[hd] + pad

    q_scr[...] = qnext_scr[...]
    q_proj(hnext_ref)

    lane = lax.broadcasted_iota(jnp.int32, (1, ATT_GROUP_LANES), 1)
    head_of_lane = lane // B_HEAD_DIM
    ones_rows = jnp.ones((ATT_ONES_ROWS, ATT_WIN), bf16)
    for h0 in range(0, B_HEADS, ATT_STAGE_HEADS):
        heads = list(range(h0, h0 + ATT_STAGE_HEADS))
        s_ts = []
        for head in heads:
            g, hh = divmod(head, ATT_GROUP)
            lanes = slice(g * ATT_GROUP_LANES, (g + 1) * ATT_GROUP_LANES)
            kwin = jnp.concatenate([k0_ref[:, lanes], k1_ref[:, lanes], k2_ref[:, lanes]], axis=0)
            keep = jnp.where(head_of_lane == hh, 1.0, 0.0).astype(bf16)
            s_ts.append(lax.dot_general(kwin, q_scr[:, lanes] * keep, (((1,), (1,)), ((), ())),
                                        preferred_element_type=f32))
        partials = []
        for head, s_t in zip(heads, s_ts):
            rows = slice(head * B_HEAD_DIM, (head + 1) * B_HEAD_DIM)
            v_t = jnp.concatenate([vt0_ref[rows, :], vt1_ref[rows, :], vt2_ref[rows, :]], axis=1)
            v_ext = jnp.concatenate([v_t, ones_rows], axis=0)
            blocks = []
            for k0 in range(0, ATT_WIN, ATT_KEY_BLOCK):
                keys = slice(k0, k0 + ATT_KEY_BLOCK)
                s_b = s_t[keys, :] + biasp_scr[head, keys, :]
                m_b = jnp.max(s_b, axis=0, keepdims=True)
                p_b = jnp.exp2(s_b - m_b).astype(bf16)
                blocks.append((m_b, jnp.dot(v_ext[:, keys], p_b, preferred_element_type=f32)))
            partials.append(blocks)
        for head, blocks in zip(heads, partials):
            rows = slice(head * B_HEAD_DIM, (head + 1) * B_HEAD_DIM)
            m = blocks[0][0]
            for m_b, _ in blocks[1:]:
                m = jnp.maximum(m, m_b)
            o_t = None
            for m_b, o_b in blocks:
                term = o_b * jnp.exp2(m_b - m)
                o_t = term if o_t is None else o_t + term
            ot_scr[rows, :] = o_t[:B_HEAD_DIM] * (1.0 / o_t[B_HEAD_DIM:B_HEAD_DIM + 1])
    mix = lax.dot_general(ot_scr[...].astype(bf16), wo_ref[...], (((0,), (0,)), ((), ())),
                          preferred_element_type=f32)
    o_ref[...] = h_ref[...] + gate_ref[...] * mix


def _attn_bias_kernel(f_ref, o_ref):
    x = jnp.broadcast_to(f_ref[...], (ATT_WIN, ATT_TILE + ATT_WIN))
    t_w = pltpu.roll(x, 0, 1, stride=1, stride_axis=0)[:, :ATT_TILE]
    w = lax.broadcasted_iota(jnp.int32, (ATT_WIN, ATT_TILE), 0)
    t = lax.broadcasted_iota(jnp.int32, (ATT_WIN, ATT_TILE), 1)
    back = t // CHUNK + B_PAST_CHUNKS - w // CHUNK
    o_ref[...] = jnp.where((back >= 0) & (back <= B_PAST_CHUNKS), t_w, MASK_VALUE)


def _attn_bias_table(rel_bias):
    n_rel, n_heads = rel_bias.shape
    span = ATT_TILE + ATT_WIN
    j = (-np.arange(span)) % span
    delta = np.where(j < ATT_WIN, j, j - span)
    idx = np.clip(delta - B_PAST_CHUNKS * CHUNK, -REL_CLIP, CHUNK - 1) + REL_CLIP
    f = (rel_bias.astype(f32) * LOG2_E)[jnp.asarray(idx)].T.reshape(n_heads, 1, span)
    return pl.pallas_call(
        _attn_bias_kernel,
        grid=(n_heads,),
        in_specs=[pl.BlockSpec((None, 1, span), lambda h: (h, 0, 0))],
        out_specs=pl.BlockSpec((None, ATT_WIN, ATT_TILE), lambda h: (h, 0, 0)),
        out_shape=jax.ShapeDtypeStruct((n_heads, ATT_WIN, ATT_TILE), f32),
        compiler_params=_params("arbitrary"),
        name="attn_bias_table",
    )(f)


def _attn_layer(h, gain, mods, layer, w_q, w_o, w_layer, k, v_t, rel_bias):
    bsz, seq, d = h.shape
    assert seq % ATT_TILE == 0 and d == B_HEADS * B_HEAD_DIM
    assert ATT_WIN == 3 * ATT_TILE
    bias = _attn_bias_table(rel_bias)
    h_spec = pl.BlockSpec((None, ATT_TILE, d), lambda b, i: (b, i, 0))

    def k_spec(back):
        return pl.BlockSpec((None, ATT_TILE, d), lambda b, i: (b, jnp.maximum(i - back, 0), 0))

    def vt_spec(back):
        return pl.BlockSpec((None, d, ATT_TILE), lambda b, i: (b, 0, jnp.maximum(i - back, 0)))

    return pl.pallas_call(
        _attn_layer_kernel,
        grid=(bsz, seq // ATT_TILE),
        in_specs=[
            h_spec,
            pl.BlockSpec((None, ATT_TILE, d), lambda b, i: (b, jnp.minimum(i + 1, seq // ATT_TILE - 1), 0)),
            _row_spec(d), _mod_spec(layer, 0, d), _mod_spec(layer, 1, d),
            _mod_spec(layer, 2, d),
            _resident((None, d, d), lambda b, i: (w_layer, 0, 0)),
            _resident((None, d, d), lambda b, i: (w_layer, 0, 0)),
            k_spec(2), k_spec(1), k_spec(0),
            vt_spec(2), vt_spec(1), vt_spec(0),
            _resident((B_HEADS, ATT_WIN, ATT_TILE), lambda b, i: (0, 0, 0)),
        ],
        out_specs=h_spec,
        out_shape=jax.ShapeDtypeStruct((bsz, seq, d), f32),
        scratch_shapes=[
            pltpu.VMEM((ATT_TILE, d), bf16),
            pltpu.VMEM((ATT_TILE, d), bf16),
            pltpu.VMEM((B_HEADS, ATT_WIN, ATT_TILE), f32),
            pltpu.VMEM((d, ATT_TILE), f32),
        ],
        compiler_params=_params("parallel", "arbitrary"),
        name="attn_layer",
    )(h, h, gain.reshape(1, d), mods, mods, mods, w_q, w_o, k, k, k, v_t, v_t, v_t, bias)


def kernel(x, c, mod_w, mod_b, norm_mix, norm_ffn, ffn_w_in, ffn_w_out, a_w_in, a_w_out, a_lb,
           a_out_norm, kv_norm, kv_mod_w, kv_mod_b, kv_w, b_w_q, b_w_o, b_rel_bias, final_norm):
    depth = mod_w.shape[0]
    n_a = a_w_in.shape[0]
    bsz = x.shape[0]

    mods = _modulation(c, mod_w, mod_b).reshape(depth, bsz, 1, -1)
    kv_mods = _modulation(c, kv_mod_w[None], kv_mod_b[None]).reshape(1, bsz, 1, -1)

    ffn_w_in_b = ffn_w_in.astype(bf16)
    ffn_w_out_b = ffn_w_out.astype(bf16)
    a_w_in_b = a_w_in.astype(bf16)
    a_w_out_b = a_w_out.astype(bf16)
    kv_w_b = kv_w.astype(bf16)
    b_w_q_b = b_w_q.astype(bf16)
    b_w_o_b = b_w_o.astype(bf16)

    h = x
    k = v_t = None
    for layer in range(depth):
        if layer < n_a:
            h = _hgrn_layer(h, norm_mix[layer], mods, layer, a_w_in_b, a_w_out_b, a_lb,
                            a_out_norm[layer])
        else:
            j = layer - n_a
            h = _attn_layer(h, norm_mix[layer], mods, layer, b_w_q_b, b_w_o_b, j, k, v_t,
                            b_rel_bias[j])
        if layer == n_a - 1:
            h, k, v_t = _ffn(h, norm_ffn[layer], mods, layer, ffn_w_in_b, ffn_w_out_b,
                         kv=(kv_norm, kv_mods, kv_w_b))
        elif layer == depth - 1:
            h = _ffn(h, norm_ffn[layer], mods, layer, ffn_w_in_b, ffn_w_out_b, final_gain=final_norm)
        else:
            h = _ffn(h, norm_ffn[layer], mods, layer, ffn_w_in_b, ffn_w_out_b)
    return h
```

```python
import functools

import numpy as np
import jax
import jax.numpy as jnp
from jax import lax
from jax.experimental import pallas as pl
from jax.experimental.pallas import tpu as pltpu

V7X_LANES = 128
V7X_SUBLANES = 8
V7X_MXU_DIM = 256
V7X_VMEM_LIMIT_BYTES = 56 * 1024 * 1024

CHUNK = 64
A_HEADS = 8
A_KEY_DIM = 128
B_HEADS = 16
B_HEAD_DIM = 64
B_PAST_CHUNKS = 8
REL_CLIP = 256
NORM_EPS = 1e-6
MASK_VALUE = -1e30
MIN_FORGET = 1e-30
LOG2_E = 1.4426950408889634

SUB = V7X_SUBLANES
BLOCKS_PER_CHUNK = CHUNK // SUB
PAIR = 2 * A_KEY_DIM

HGRN_TILE = 512
ATT_TILE = 256
ATT_STEP_TILES = 2
ATT_WIN = ATT_TILE + B_PAST_CHUNKS * CHUNK
ATT_GROUP = 4
ATT_GROUP_LANES = ATT_GROUP * B_HEAD_DIM
ATT_STAGE_HEADS = 16
ATT_KEY_BLOCK = 128
ATT_ONES_ROWS = 16
ROW_TILE = 1024
ROW_TILE_KV = 512
MOD_MAX_TILE = 2048

f32 = jnp.float32
bf16 = jnp.bfloat16


def _params(*sem):
    return pltpu.CompilerParams(dimension_semantics=sem, vmem_limit_bytes=V7X_VMEM_LIMIT_BYTES)


def _resident(shape, index_map):
    return pl.BlockSpec(shape, index_map, pipeline_mode=pl.Buffered(1))


def _row_spec(d):
    return pl.BlockSpec((1, d), lambda b, i: (0, 0))


def _mod_spec(layer, col, d):
    return pl.BlockSpec((None, None, 1, d), lambda b, i: (layer, b, 0, col))


def _sigmoid(x):
    return 0.5 * jnp.tanh(0.5 * x) + 0.5


def _norm_mod(x, gain, shift, scale):
    y = x * lax.rsqrt(jnp.mean(x * x, axis=-1, keepdims=True) + NORM_EPS)
    return (y * gain) * (1.0 + scale) + shift


def _mod_kernel(c_ref, w_ref, b_ref, o_ref):
    c = c_ref[...]
    c_act = c * _sigmoid(c)
    o_ref[...] = jnp.dot(c_act, w_ref[...], preferred_element_type=f32,
                         precision=lax.Precision.HIGHEST) + b_ref[...]


def _modulation(c, w, b):
    n_layers, d, n = w.shape
    bsz = c.shape[0]
    tn = max(t for t in range(V7X_LANES, MOD_MAX_TILE + 1, V7X_LANES) if n % t == 0)
    return pl.pallas_call(
        _mod_kernel,
        grid=(n_layers, n // tn),
        in_specs=[
            pl.BlockSpec((bsz, d), lambda l, j: (0, 0)),
            pl.BlockSpec((None, d, tn), lambda l, j: (l, 0, j)),
            pl.BlockSpec((None, 1, tn), lambda l, j: (l, 0, j)),
        ],
        out_specs=pl.BlockSpec((None, bsz, tn), lambda l, j: (l, 0, j)),
        out_shape=jax.ShapeDtypeStruct((n_layers, bsz, n), f32),
        compiler_params=_params("arbitrary", "arbitrary"),
        name="modulation",
    )(c, w, b.reshape(n_layers, 1, n))


def _ffn_kernel(*refs, ffn_dim, final_norm, with_kv):
    h_ref, gain_ref, shift_ref, scale_ref, gate_ref, win_ref, wout_ref = refs[:7]
    rest = refs[7:]
    x = h_ref[...]
    u = _norm_mod(x, gain_ref[...], shift_ref[...], scale_ref[...]).astype(bf16)
    mid = (ffn_dim // 2) // V7X_MXU_DIM * V7X_MXU_DIM
    acc = None
    for lo, hi in ((0, mid), (mid, ffn_dim)):
        a = jnp.dot(u, win_ref[:, lo:hi], preferred_element_type=f32)
        b = jnp.dot(u, win_ref[:, ffn_dim + lo:ffn_dim + hi], preferred_element_type=f32)
        act = ((a * _sigmoid(a)) * b).astype(bf16)
        p = jnp.dot(act, wout_ref[lo:hi, :], preferred_element_type=f32)
        acc = p if acc is None else acc + p
    out = x + gate_ref[...] * acc
    if with_kv:
        kgain_ref, kshift_ref, kscale_ref, kvw_ref, o_ref, k_ref, vt_ref = rest
        ukv = _norm_mod(out, kgain_ref[...], kshift_ref[...], kscale_ref[...]).astype(bf16)
        kv = jnp.dot(ukv, kvw_ref[...], preferred_element_type=f32)
        d_kv = k_ref.shape[-1]
        k_ref[...] = kv[:, :d_kv].astype(k_ref.dtype)
        vt_ref[...] = kv[:, d_kv:].T.astype(vt_ref.dtype)
    elif final_norm:
        fgain_ref, o_ref = rest
        out = out * lax.rsqrt(jnp.mean(out * out, axis=-1, keepdims=True) + NORM_EPS)
        out = out * fgain_ref[...]
    else:
        (o_ref,) = rest
    o_ref[...] = out


def _ffn(h, gain, mods, layer, w_in, w_out, final_gain=None, kv=None):
    bsz, seq, d = h.shape
    ffn_dim = w_out.shape[1]
    tm = min(ROW_TILE if kv is None else ROW_TILE_KV, seq)
    assert seq % tm == 0 and ffn_dim % V7X_LANES == 0
    h_spec = pl.BlockSpec((None, tm, d), lambda b, i: (b, i, 0))
    in_specs = [
        h_spec, _row_spec(d), _mod_spec(layer, 3, d), _mod_spec(layer, 4, d), _mod_spec(layer, 5, d),
        _resident((None, d, 2 * ffn_dim), lambda b, i: (layer, 0, 0)),
        _resident((None, ffn_dim, d), lambda b, i: (layer, 0, 0)),
    ]
    args = [h, gain.reshape(1, d), mods, mods, mods, w_in, w_out]
    out_specs = h_spec
    out_shape = jax.ShapeDtypeStruct((bsz, seq, d), f32)
    if kv is not None:
        kv_norm, kv_mods, kv_w = kv
        n_kv = kv_w.shape[-1]
        in_specs += [_row_spec(d), _mod_spec(0, 0, d), _mod_spec(0, 1, d),
                     _resident((d, n_kv), lambda b, i: (0, 0))]
        args += [kv_norm.reshape(1, d), kv_mods, kv_mods, kv_w]
        out_specs = (h_spec, pl.BlockSpec((None, tm, n_kv // 2), lambda b, i: (b, i, 0)),
                     pl.BlockSpec((None, n_kv // 2, tm), lambda b, i: (b, 0, i)))
        out_shape = (out_shape, jax.ShapeDtypeStruct((bsz, seq, n_kv // 2), bf16),
                     jax.ShapeDtypeStruct((bsz, n_kv // 2, seq), bf16))
    elif final_gain is not None:
        in_specs += [_row_spec(d)]
        args += [final_gain.reshape(1, d)]
    return pl.pallas_call(
        functools.partial(_ffn_kernel, ffn_dim=ffn_dim, final_norm=final_gain is not None,
                          with_kv=kv is not None),
        grid=(bsz, seq // tm),
        in_specs=in_specs,
        out_specs=out_specs,
        out_shape=out_shape,
        compiler_params=_params("parallel", "arbitrary"),
        name="ffn",
    )(*args)


def _block_diag2(a, b):
    z = jnp.zeros_like(a)
    return jnp.concatenate([jnp.concatenate([a, z], axis=1),
                            jnp.concatenate([z, b], axis=1)], axis=0)


def _block_cumsum(x3):
    sub = lax.broadcasted_iota(jnp.int32, x3.shape, 1)
    for sh in (1, 2, 4):
        x3 = x3 + jnp.where(sub >= sh, pltpu.roll(x3, sh, axis=1), 0.0)
    return x3


def _hgrn_layer_kernel(h_ref, hnext_ref, gain_ref, shift_ref, scale_ref, gate_ref, win_ref, wout_ref,
                       alb_ref, ogain_ref, wsel_ref, wselw_ref, ones_ref, o_ref,
                       st_ref, y_scr, q_scr, w_scr, c_scr, qt_scr, kt_scr, vb_scr, sg_scr, ctmp_scr,
                       ctot_scr, ginc_scr, lhs_scr, diag_scr, o_scr, *, layer, tile):
    d = A_HEADS * A_KEY_DIM
    nb = tile // SUB
    n_chunks = tile // CHUNK
    n_pairs = A_HEADS // 2

    def in_proj(src_ref):
        u = _norm_mod(src_ref[...], gain_ref[...], shift_ref[...], scale_ref[...]).astype(bf16)
        y_scr[...] = jnp.dot(u, win_ref[...], preferred_element_type=f32)

    @pl.when(pl.program_id(1) == 0)
    def _():
        st_ref[...] = jnp.zeros_like(st_ref)
        in_proj(h_ref)

    a_lb = alb_ref[...]
    e_lb = jnp.exp(a_lb - jnp.max(a_lb, axis=0, keepdims=True))
    sm = e_lb / jnp.sum(e_lb, axis=0, keepdims=True)
    lb = jnp.zeros((1, d), f32)
    for m in range(1, layer + 1):
        lb = lb + sm[m:m + 1, :]
    oml = 1.0 - lb

    qpre = y_scr[:, 0:d]
    zf = y_scr[:, d:2 * d]
    q = qpre * _sigmoid(qpre)
    sig = _sigmoid(zf)
    log_f = jnp.log2(jnp.maximum(lb + oml * sig, MIN_FORGET))
    k_sign = jnp.where(oml < 0.0, -1.0, 1.0)
    log_k = jnp.log2(jnp.abs(oml) * (1.0 - sig))

    c3 = _block_cumsum(log_f.reshape(nb, SUB, d))
    ctot3 = c3[:, SUB - 1:SUB, :]
    w3 = c3 - log_k.reshape(nb, SUB, d)
    q_scr[...] = (q * k_sign).astype(bf16)
    c_scr[...] = c3.reshape(tile, d)
    w_scr[...] = w3.reshape(tile, d)
    qt_scr[...] = (q.reshape(nb, SUB, d) * jnp.exp2(c3)).reshape(tile, d)
    kt_scr[...] = (jnp.exp2(ctot3 - w3).reshape(tile, d)) * k_sign
    vb_scr[...] = y_scr[:, 2 * d:3 * d].astype(bf16)
    g = y_scr[:, 3 * d:4 * d]
    sg_scr[...] = g * _sigmoid(g)
    in_proj(hnext_ref)

    for h in range(d // V7X_LANES):
        ctmp_scr[h] = c_scr[:, h * V7X_LANES:(h + 1) * V7X_LANES]
    ctot = jnp.concatenate([ctmp_scr[h, pl.ds(SUB - 1, nb, stride=SUB), :]
                            for h in range(d // V7X_LANES)], axis=1)
    ctot_scr[...] = ctot
    ginc_scr[...] = _block_cumsum(ctot.reshape(n_chunks, BLOCKS_PER_CHUNK, d)).reshape(nb, d)

    sub = lax.broadcasted_iota(jnp.int32, (nb, SUB, d), 1)
    for s in range(SUB):
        c3 = c_scr[...].reshape(nb, SUB, d)
        w_s = w_scr[...].reshape(nb, SUB, d)[:, s:s + 1, :]
        decay = jnp.exp2(c3 - w_s if s == 0 else jnp.where(sub >= s, c3 - w_s, MASK_VALUE))
        p = decay.reshape(tile, d).astype(bf16) * q_scr[...]
        for m in range(n_pairs):
            lhs_scr[m * tile:(m + 1) * tile, s * PAIR:(s + 1) * PAIR] = p[:, m * PAIR:(m + 1) * PAIR]
    half = n_pairs * tile // 2
    diag_scr[0:half] = jnp.dot(lhs_scr[0:half], wsel_ref[...], preferred_element_type=f32)
    diag_scr[half:] = jnp.dot(lhs_scr[half:], wselw_ref[...],
                              preferred_element_type=f32)[:, :2 * CHUNK]

    lane = lax.broadcasted_iota(jnp.int32, (SUB, V7X_LANES), 1)
    row = lax.broadcasted_iota(jnp.int32, (SUB, V7X_LANES), 0)
    col_block = (lane % CHUNK) // SUB
    col_sub = lane % SUB

    units = [(ci, m) for ci in range(n_chunks) for m in range(n_pairs)]
    per_chunk = []
    for ci in range(n_chunks):
        r0, b0 = ci * CHUNK, ci * BLOCKS_PER_CHUNK
        qt = qt_scr[r0:r0 + CHUNK, :]
        kt = kt_scr[r0:r0 + CHUNK, :]
        ginc = ginc_scr[b0:b0 + BLOCKS_PER_CHUNK, :]
        gexc = ginc - ctot_scr[b0:b0 + BLOCKS_PER_CHUNK, :]
        btot = ginc[BLOCKS_PER_CHUNK - 1:BLOCKS_PER_CHUNK, :]
        e_g = jnp.exp2(gexc)
        e_r = jnp.exp2(btot - ginc)
        qh = jnp.concatenate([qt[SUB * i:SUB * (i + 1)] * e_g[i:i + 1] for i in range(BLOCKS_PER_CHUNK)],
                             axis=0).astype(bf16)
        kh = jnp.concatenate([kt[SUB * i:SUB * (i + 1)] * e_r[i:i + 1] for i in range(BLOCKS_PER_CHUNK)],
                             axis=0).astype(bf16)
        rows = []
        for i in range(1, BLOCKS_PER_CHUNK):
            d_i = jnp.exp2(gexc[i:i + 1] - ginc[0:i])
            for j in range(i):
                rows.append(qt[SUB * i:SUB * (i + 1)] * d_i[j:j + 1])
        lhs_off = jnp.concatenate(rows, axis=0).astype(bf16)
        per_chunk.append(dict(qh=qh, kh=kh, lhs_off=lhs_off, ktb=kt.astype(bf16),
                              vb=vb_scr[r0:r0 + CHUNK, :], e_b=jnp.exp2(btot)))

    def pair_lanes(m):
        return m * PAIR, m * PAIR + A_KEY_DIM, (m + 1) * PAIR

    offs = {}
    for ci, m in units:
        lo, mid, hi = pair_lanes(m)
        c = per_chunk[ci]
        k_bd = _block_diag2(c["ktb"][:, lo:mid], c["ktb"][:, mid:hi])
        offs[ci, m] = lax.dot_general(c["lhs_off"][:, lo:hi], k_bd, (((1,), (1,)), ((), ())),
                                      preferred_element_type=f32)
    a2s = {}
    for ci, m in units:
        off = offs[ci, m]
        dg = diag_scr[m * tile + ci * CHUNK:m * tile + (ci + 1) * CHUNK, :]
        blocks = []
        for i in range(BLOCKS_PER_CHUNK):
            a = jnp.where(col_block == i, jnp.where(col_sub <= row, dg[SUB * i:SUB * (i + 1)], 0.0), 0.0)
            for j in range(i):
                pidx = i * (i - 1) // 2 + j
                a = jnp.where(col_block == j, off[SUB * pidx:SUB * (pidx + 1)], a)
            blocks.append(a)
        a2s[ci, m] = jnp.concatenate(blocks, axis=0).astype(bf16)
    for ci, m in units:
        lo, mid, hi = pair_lanes(m)
        vb = per_chunk[ci]["vb"]
        v_bd = _block_diag2(vb[:, lo:mid], vb[:, mid:hi])
        o_scr[ci * CHUNK:(ci + 1) * CHUNK, lo:hi] = jnp.dot(a2s[ci, m], v_bd,
                                                            preferred_element_type=f32)

    for ci in range(n_chunks):
        c = per_chunk[ci]
        for m in range(n_pairs):
            lo, mid, hi = pair_lanes(m)
            st_e = st_ref[2 * m]
            st_o = st_ref[2 * m + 1]
            st_bd = _block_diag2(st_e.astype(bf16), st_o.astype(bf16))
            o_inter = lax.dot_general(c["qh"][:, lo:hi], st_bd, (((1,), (1,)), ((), ())),
                                      preferred_element_type=f32)
            o_scr[ci * CHUNK:(ci + 1) * CHUNK, lo:hi] += o_inter
            upd = lax.dot_general(c["vb"][:, lo:hi], c["kh"][:, lo:hi], (((0,), (0,)), ((), ())),
                                  preferred_element_type=f32)
            st_ref[2 * m] = c["e_b"][:, lo:mid] * st_e + upd[:A_KEY_DIM, :A_KEY_DIM]
            st_ref[2 * m + 1] = c["e_b"][:, mid:hi] * st_o + upd[A_KEY_DIM:, A_KEY_DIM:]

    o = o_scr[...]
    osq = (o * o).astype(bf16)
    ms = jnp.concatenate(
        [jnp.dot(osq[:, m * PAIR:(m + 1) * PAIR], ones_ref[...], preferred_element_type=f32)
         for m in range(n_pairs)], axis=1) * (1.0 / A_KEY_DIM)
    out = o * lax.rsqrt(ms + NORM_EPS) * ogain_ref[...]
    out = (out * sg_scr[...]).astype(bf16)
    mix = jnp.dot(out, wout_ref[...], preferred_element_type=f32)
    o_ref[...] = h_ref[...] + gate_ref[...] * mix


def _hgrn_selector():
    r = np.arange(SUB * PAIR)
    s_local, h_r = r // PAIR, (r % PAIR) // A_KEY_DIM
    c = np.arange(2 * CHUNK)
    h_c, s_c = c // CHUNK, c % CHUNK
    sel = (h_r[:, None] == h_c[None, :]) & (s_local[:, None] == (s_c % SUB)[None, :])
    return jnp.asarray(sel, dtype=bf16)


def _hgrn_layer(h, gain, mods, layer, w_in, w_out, a_lb, o_gain):
    bsz, seq, d = h.shape
    tile = min(HGRN_TILE, seq)
    assert d == A_HEADS * A_KEY_DIM and seq % tile == 0 and tile % CHUNK == 0
    n_pairs = A_HEADS // 2
    n_in = w_in.shape[-1]
    ones_bd = jnp.asarray(np.kron(np.eye(2), np.ones((A_KEY_DIM, A_KEY_DIM))), dtype=bf16)
    gain_t = jnp.tile(o_gain.astype(f32), A_HEADS).reshape(1, d)
    n_a = a_lb.shape[0]
    sel = _hgrn_selector()
    h_spec = pl.BlockSpec((None, tile, d), lambda b, i: (b, i, 0))
    return pl.pallas_call(
        functools.partial(_hgrn_layer_kernel, layer=layer, tile=tile),
        grid=(bsz, seq // tile),
        in_specs=[
            h_spec,
            pl.BlockSpec((None, tile, d), lambda b, i: (b, jnp.minimum(i + 1, seq // tile - 1), 0)),
            _row_spec(d), _mod_spec(layer, 0, d), _mod_spec(layer, 1, d),
            _mod_spec(layer, 2, d),
            _resident((None, d, n_in), lambda b, i: (layer, 0, 0)),
            _resident((None, d, d), lambda b, i: (layer, 0, 0)),
            pl.BlockSpec((n_a, d), lambda b, i: (0, 0)),
            _row_spec(d),
            _resident((SUB * PAIR, 2 * CHUNK), lambda b, i: (0, 0)),
            _resident((SUB * PAIR, 4 * CHUNK), lambda b, i: (0, 0)),
            _resident((PAIR, PAIR), lambda b, i: (0, 0)),
        ],
        out_specs=h_spec,
        out_shape=jax.ShapeDtypeStruct((bsz, seq, d), f32),
        scratch_shapes=[
            pltpu.VMEM((A_HEADS, A_KEY_DIM, A_KEY_DIM), f32),
            pltpu.VMEM((tile, n_in), f32),
            pltpu.VMEM((tile, d), bf16),
            pltpu.VMEM((tile, d), f32),
            pltpu.VMEM((tile, d), f32),
            pltpu.VMEM((tile, d), f32),
            pltpu.VMEM((tile, d), f32),
            pltpu.VMEM((tile, d), bf16),
            pltpu.VMEM((tile, d), f32),
            pltpu.VMEM((d // V7X_LANES, tile, V7X_LANES), f32),
            pltpu.VMEM((tile // SUB, d), f32),
            pltpu.VMEM((tile // SUB, d), f32),
            pltpu.VMEM((n_pairs * tile, SUB * PAIR), bf16),
            pltpu.VMEM((n_pairs * tile, 2 * CHUNK), f32),
            pltpu.VMEM((tile, d), f32),
        ],
        compiler_params=_params("parallel", "arbitrary"),
        name="hgrn_layer",
    )(h, h, gain.reshape(1, d), mods, mods, mods, w_in, w_out, a_lb.astype(f32), gain_t,
      sel, jnp.concatenate([sel, jnp.zeros_like(sel)], axis=1), ones_bd)


def _attn_layer_kernel(h_ref, hnext_ref, gain_ref, shift_ref, scale_ref, gate_ref, wq_ref, wo_ref,
                       k0_ref, k1_ref, k2_ref, k3_ref, vt0_ref, vt1_ref, vt2_ref, vt3_ref,
                       bias_ref, o_ref, q_scr, qnext_scr, ot_scr):
    step = pl.program_id(1)
    k_refs = (k0_ref, k1_ref, k2_ref, k3_ref)
    vt_refs = (vt0_ref, vt1_ref, vt2_ref, vt3_ref)

    def q_proj(src_ref, rows):
        u = _norm_mod(src_ref[rows, :], gain_ref[...], shift_ref[...], scale_ref[...]).astype(bf16)
        q = jnp.dot(u, wq_ref[...], preferred_element_type=f32) * (B_HEAD_DIM ** -0.5 * LOG2_E)
        qnext_scr[...] = q.astype(bf16)

    def tile_rows(t):
        return slice(t * ATT_TILE, (t + 1) * ATT_TILE)

    @pl.when(step == 0)
    def _():
        q_proj(h_ref, tile_rows(0))

    lane = lax.broadcasted_iota(jnp.int32, (1, ATT_GROUP_LANES), 1)
    head_of_lane = lane // B_HEAD_DIM
    ones_rows = jnp.ones((ATT_ONES_ROWS, ATT_WIN), bf16)
    for t in range(ATT_STEP_TILES):
        i = step * ATT_STEP_TILES + t
        k_win, vt_win = k_refs[t:t + 3], vt_refs[t:t + 3]
        q_scr[...] = qnext_scr[...]
        if t + 1 < ATT_STEP_TILES:
            q_proj(h_ref, tile_rows(t + 1))
        else:
            q_proj(hnext_ref, tile_rows(0))
        first_valid_row = (ATT_WIN // ATT_TILE - 1 - i) * ATT_TILE

        for h0 in range(0, B_HEADS, ATT_STAGE_HEADS):
            heads = list(range(h0, h0 + ATT_STAGE_HEADS))
            s_ts = []
            for head in heads:
                g, hh = divmod(head, ATT_GROUP)
                lanes = slice(g * ATT_GROUP_LANES, (g + 1) * ATT_GROUP_LANES)
                kwin = jnp.concatenate([r[:, lanes] for r in k_win], axis=0)
                keep = jnp.where(head_of_lane == hh, 1.0, 0.0).astype(bf16)
                s_ts.append(lax.dot_general(kwin, q_scr[:, lanes] * keep, (((1,), (1,)), ((), ())),
                                            preferred_element_type=f32))
            partials = []
            for head, s_t in zip(heads, s_ts):
                rows = slice(head * B_HEAD_DIM, (head + 1) * B_HEAD_DIM)
                v_t = jnp.concatenate([r[rows, :] for r in vt_win], axis=1)
                v_ext = jnp.concatenate([v_t, ones_rows], axis=0)
                blocks = []
                for k0 in range(0, ATT_WIN, ATT_KEY_BLOCK):
                    keys = slice(k0, k0 + ATT_KEY_BLOCK)
                    s_b = s_t[keys, :] + bias_ref[head, keys, :]
                    m_b = jnp.max(s_b, axis=0, keepdims=True)
                    p_b = jnp.exp2(s_b - m_b).astype(bf16)
                    m_b = jnp.where(k0 >= first_valid_row, m_b, MASK_VALUE)
                    blocks.append((m_b, jnp.dot(v_ext[:, keys], p_b, preferred_element_type=f32)))
                partials.append(blocks)
            for head, blocks in zip(heads, partials):
                rows = slice(head * B_HEAD_DIM, (head + 1) * B_HEAD_DIM)
                m = blocks[0][0]
                for m_b, _ in blocks[1:]:
                    m = jnp.maximum(m, m_b)
                o_t = None
                for m_b, o_b in blocks:
                    term = o_b * jnp.exp2(m_b - m)
                    o_t = term if o_t is None else o_t + term
                ot_scr[rows, :] = o_t[:B_HEAD_DIM] * (1.0 / o_t[B_HEAD_DIM:B_HEAD_DIM + 1])
        mix = lax.dot_general(ot_scr[...].astype(bf16), wo_ref[...], (((0,), (0,)), ((), ())),
                              preferred_element_type=f32)
        o_ref[tile_rows(t), :] = h_ref[tile_rows(t), :] + gate_ref[...] * mix


def _attn_bias_kernel(f_ref, o_ref):
    x = jnp.broadcast_to(f_ref[...], (ATT_WIN, ATT_TILE + ATT_WIN))
    t_w = pltpu.roll(x, 0, 1, stride=1, stride_axis=0)[:, :ATT_TILE]
    w = lax.broadcasted_iota(jnp.int32, (ATT_WIN, ATT_TILE), 0)
    t = lax.broadcasted_iota(jnp.int32, (ATT_WIN, ATT_TILE), 1)
    back = t // CHUNK + B_PAST_CHUNKS - w // CHUNK
    o_ref[...] = jnp.where((back >= 0) & (back <= B_PAST_CHUNKS), t_w, MASK_VALUE)


def _attn_bias_table(rel_bias):
    n_rel, n_heads = rel_bias.shape
    span = ATT_TILE + ATT_WIN
    j = (-np.arange(span)) % span
    delta = np.where(j < ATT_WIN, j, j - span)
    idx = np.clip(delta - B_PAST_CHUNKS * CHUNK, -REL_CLIP, CHUNK - 1) + REL_CLIP
    f = (rel_bias.astype(f32) * LOG2_E)[jnp.asarray(idx)].T.reshape(n_heads, 1, span)
    return pl.pallas_call(
        _attn_bias_kernel,
        grid=(n_heads,),
        in_specs=[pl.BlockSpec((None, 1, span), lambda h: (h, 0, 0))],
        out_specs=pl.BlockSpec((None, ATT_WIN, ATT_TILE), lambda h: (h, 0, 0)),
        out_shape=jax.ShapeDtypeStruct((n_heads, ATT_WIN, ATT_TILE), f32),
        compiler_params=_params("arbitrary"),
        name="attn_bias_table",
    )(f)


def _attn_layer(h, gain, mods, layer, w_q, w_o, w_layer, k, v_t, rel_bias):
    bsz, seq, d = h.shape
    step_rows = ATT_STEP_TILES * ATT_TILE
    n_steps = seq // step_rows
    assert seq % step_rows == 0 and d == B_HEADS * B_HEAD_DIM
    assert ATT_WIN == 3 * ATT_TILE
    bias = _attn_bias_table(rel_bias)
    h_spec = pl.BlockSpec((None, step_rows, d), lambda b, i: (b, i, 0))

    def key_tile(j):
        return lambda i: jnp.maximum(i * ATT_STEP_TILES - 2 + j, 0)

    def k_spec(j):
        return pl.BlockSpec((None, ATT_TILE, d), lambda b, i: (b, key_tile(j)(i), 0))

    def vt_spec(j):
        return pl.BlockSpec((None, d, ATT_TILE), lambda b, i: (b, 0, key_tile(j)(i)))

    return pl.pallas_call(
        _attn_layer_kernel,
        grid=(bsz, n_steps),
        in_specs=[
            h_spec,
            pl.BlockSpec((None, step_rows, d), lambda b, i: (b, jnp.minimum(i + 1, n_steps - 1), 0)),
            _row_spec(d), _mod_spec(layer, 0, d), _mod_spec(layer, 1, d),
            _mod_spec(layer, 2, d),
            _resident((None, d, d), lambda b, i: (w_layer, 0, 0)),
            _resident((None, d, d), lambda b, i: (w_layer, 0, 0)),
            k_spec(0), k_spec(1), k_spec(2), k_spec(3),
            vt_spec(0), vt_spec(1), vt_spec(2), vt_spec(3),
            _resident((B_HEADS, ATT_WIN, ATT_TILE), lambda b, i: (0, 0, 0)),
        ],
        out_specs=h_spec,
        out_shape=jax.ShapeDtypeStruct((bsz, seq, d), f32),
        scratch_shapes=[
            pltpu.VMEM((ATT_TILE, d), bf16),
            pltpu.VMEM((ATT_TILE, d), bf16),
            pltpu.VMEM((d, ATT_TILE), f32),
        ],
        compiler_params=_params("parallel", "arbitrary"),
        name="attn_layer",
    )(h, h, gain.reshape(1, d), mods, mods, mods, w_q, w_o, k, k, k, k, v_t, v_t, v_t, v_t, bias)


def kernel(x, c, mod_w, mod_b, norm_mix, norm_ffn, ffn_w_in, ffn_w_out, a_w_in, a_w_out, a_lb,
           a_out_norm, kv_norm, kv_mod_w, kv_mod_b, kv_w, b_w_q, b_w_o, b_rel_bias, final_norm):
    depth = mod_w.shape[0]
    n_a = a_w_in.shape[0]
    bsz = x.shape[0]

    mods = _modulation(c, mod_w, mod_b).reshape(depth, bsz, 1, -1)
    kv_mods = _modulation(c, kv_mod_w[None], kv_mod_b[None]).reshape(1, bsz, 1, -1)

    ffn_w_in_b = ffn_w_in.astype(bf16)
    ffn_w_out_b = ffn_w_out.astype(bf16)
    a_w_in_b = a_w_in.astype(bf16)
    a_w_out_b = a_w_out.astype(bf16)
    kv_w_b = kv_w.astype(bf16)
    b_w_q_b = b_w_q.astype(bf16)
    b_w_o_b = b_w_o.astype(bf16)

    h = x
    k = v_t = None
    for layer in range(depth):
        if layer < n_a:
            h = _hgrn_layer(h, norm_mix[layer], mods, layer, a_w_in_b, a_w_out_b, a_lb,
                            a_out_norm[layer])
        else:
            j = layer - n_a
            h = _attn_layer(h, norm_mix[layer], mods, layer, b_w_q_b, b_w_o_b, j, k, v_t,
                            b_rel_bias[j])
        if layer == n_a - 1:
            h, k, v_t = _ffn(h, norm_ffn[layer], mods, layer, ffn_w_in_b, ffn_w_out_b,
                         kv=(kv_norm, kv_mods, kv_w_b))
        elif layer == depth - 1:
            h = _ffn(h, norm_ffn[layer], mods, layer, ffn_w_in_b, ffn_w_out_b, final_gain=final_norm)
        else:
            h = _ffn(h, norm_ffn[layer], mods, layer, ffn_w_in_b, ffn_w_out_b)
    return h
```
